```python
import jax, jax.numpy as jnp
from jax import lax
import numpy as np

D_MODEL = 2048
BATCH = 4
SEQ = 2048
DEPTH = 4
DEC_BATCH = 128
DEC_SEQ = 4
PAST_LEN = 16384
PAGE_SIZE = 128

N_EVEN = (DEPTH + 1) // 2
N_ODD = DEPTH // 2
POOL_WINDOWS = (2, 4, 8, 16)
POOL_GROUPS = len(POOL_WINDOWS)
A_WIDTH = D_MODEL // 2
POOL_GROUP_DIM = A_WIDTH // POOL_GROUPS
POOL_BUF = max(POOL_WINDOWS) - 1
B_WIDTH = D_MODEL // 2
SGU_HEADS = 4
SGU_HEAD_DIM = B_WIDTH // SGU_HEADS
SGU_CHUNK = 128
MIX_IN = A_WIDTH + 2 * B_WIDTH
MIX_OUT = A_WIDTH + B_WIDTH
RET_HEADS = 8
RET_DK = D_MODEL // RET_HEADS
RET_DV = 2 * RET_DK
RET_CHUNK = 128
ROPE_BASE = 10000.0
D_FF = 256 * ((8 * D_MODEL // 3 + 255) // 256)
CONV_W = 3
EPS = 1e-6

kernel_name = 'hybrid_pool_sgu_retention_decoder_step'

F32 = jnp.float32


def rms_norm(x, g):
    xf = x.astype(F32)
    y = xf * lax.rsqrt(jnp.mean(xf * xf, axis=-1, keepdims=True) + EPS)
    return (y * g.astype(F32)).astype(x.dtype)


def layer_norm(x, g, b=None):
    xf = x.astype(F32)
    mu = jnp.mean(xf, axis=-1, keepdims=True)
    xc = xf - mu
    y = xc * lax.rsqrt(jnp.mean(xc * xc, axis=-1, keepdims=True) + EPS) * g.astype(F32)
    if b is not None:
        y = y + b.astype(F32)
    return y


def rotary(x, pos):
    half = x.shape[-1] // 2
    inv = ROPE_BASE ** (-jnp.arange(half, dtype=F32) / half)
    ang = pos[:, None] * inv[None, :]
    cos = jnp.cos(ang)[None, :, None, :]
    sin = jnp.sin(ang)[None, :, None, :]
    xf = x.astype(F32)
    x1, x2 = xf[..., :half], xf[..., half:]
    return jnp.concatenate([x1 * cos - x2 * sin, x1 * sin + x2 * cos], axis=-1)


def pool_sgu_mixer(h, pool_buf, pos0, w_in, w_grp, pool_scale, w_s, b_s, sgu_g, sgu_b, w_out):
    bn, L, _ = h.shape
    z = h @ w_in
    a = z[..., :A_WIDTH]
    u = jax.nn.gelu(z[..., A_WIDTH:A_WIDTH + B_WIDTH], approximate=True)
    v = jax.nn.gelu(z[..., A_WIDTH + B_WIDTH:], approximate=True)
    ext = jnp.concatenate([pool_buf.astype(a.dtype), a], axis=1)
    cs = jnp.pad(jnp.cumsum(ext.astype(F32), axis=1), ((0, 0), (1, 0), (0, 0)))
    pos = pos0 + jnp.arange(L)
    end = POOL_BUF + 1
    pooled = []
    for gi, w in enumerate(POOL_WINDOWS):
        c0, c1 = gi * POOL_GROUP_DIM, (gi + 1) * POOL_GROUP_DIM
        s = cs[:, end:end + L, c0:c1] - cs[:, end - w:end - w + L, c0:c1]
        cnt = jnp.minimum(pos + 1, w).astype(F32)[None, :, None]
        pooled.append(s / cnt)
    pooled = jnp.concatenate(pooled, axis=-1)
    d = (pooled - a.astype(F32)).astype(a.dtype).reshape(bn, L, POOL_GROUPS, POOL_GROUP_DIM)
    a_out = jnp.einsum('blgc,gcd->blgd', d, w_grp).reshape(bn, L, A_WIDTH) * pool_scale
    new_pool = ext[:, -POOL_BUF:]
    vn = layer_norm(v, sgu_g, sgu_b).astype(v.dtype)
    C = min(L, SGU_CHUNK)
    n = L // C
    mask = jnp.tril(jnp.ones((C, C), dtype=bool))
    ws = jnp.where(mask, w_s[:, :C, :C], 0.0)
    vh = vn.reshape(bn, n, C, SGU_HEADS, SGU_HEAD_DIM)
    mixed = jnp.einsum('hij,bnjhd->bnihd', ws, vh) + b_s[:, :C].T[:, :, None]
    b_out = u * mixed.reshape(bn, L, B_WIDTH)
    y = jnp.concatenate([a_out, b_out], axis=-1) @ w_out
    return y, new_pool, vn


def retention_mixer(h, S0, pos0, w_q, w_k, w_v, w_g, gn_g, w_o):
    bn, L, _ = h.shape
    pos = (pos0 + jnp.arange(L)).astype(F32)
    q = rotary((h @ w_q).reshape(bn, L, RET_HEADS, RET_DK), pos)
    k = rotary((h @ w_k).reshape(bn, L, RET_HEADS, RET_DK), pos) * (RET_DK ** -0.5)
    v = (h @ w_v).reshape(bn, L, RET_HEADS, RET_DV).astype(F32)
    C = min(L, RET_CHUNK)
    n = L // C
    lg = jnp.log1p(-jnp.exp2(-5.0 - jnp.arange(RET_HEADS, dtype=F32)))
    idx = jnp.arange(C, dtype=F32)
    diff = idx[:, None] - idx[None, :]
    dmask = jnp.where(diff >= 0, jnp.exp(lg[:, None, None] * jnp.maximum(diff, 0.0)), 0.0)
    xi = jnp.exp(lg[:, None] * (idx + 1.0)).T
    zeta = jnp.exp(lg[:, None] * (C - 1.0 - idx)).T
    g_c = jnp.exp(lg * C)

    def step(S, blk):
        qc, kc, vc = blk
        sc = jnp.einsum('bihd,bjhd->bhij', qc, kc) * dmask
        o = jnp.einsum('bhij,bjhe->bihe', sc, vc) + jnp.einsum('bihd,bhde->bihe', qc, S) * xi[None, :, :, None]
        S = g_c[None, :, None, None] * S + jnp.einsum('bjhd,bjhe->bhde', kc * zeta[None, :, :, None], vc)
        return S, o

    def to_blocks(t):
        return t.reshape(bn, n, C, RET_HEADS, t.shape[-1]).swapaxes(0, 1)

    S_fin, o = lax.scan(step, S0.astype(F32), (to_blocks(q), to_blocks(k), to_blocks(v)))
    o = o.swapaxes(0, 1).reshape(bn, L, RET_HEADS, RET_DV)
    o = layer_norm(o, gn_g).reshape(bn, L, RET_HEADS * RET_DV).astype(h.dtype)
    y = (jax.nn.silu(h @ w_g) * o) @ w_o
    return y, S_fin.astype(S0.dtype)


def conv_ffn(h, buf, w_gate, w_up, w_conv, b_conv, w_down):
    L = h.shape[1]
    gate = h @ w_gate
    ext = jnp.concatenate([buf.astype(gate.dtype), gate], axis=1)
    conv = b_conv
    for i in range(CONV_W):
        conv = conv + ext[:, i:i + L] * w_conv[i]
    y = (jax.nn.gelu(conv, approximate=True) * (h @ w_up)) @ w_down
    return y, ext[:, -(CONV_W - 1):]


def setup_inputs(seed: int = 0) -> dict:
    key = jax.random.key(seed)
    ks = iter(jax.random.split(key, 40))

    def nrm(shape, scale):
        return jax.random.normal(next(ks), shape, F32) * scale

    def gain(shape):
        return 1.0 + 0.1 * jax.random.normal(next(ks), shape, F32)

    return {
        'x_prompt': nrm((BATCH, SEQ, D_MODEL), 1.0),
        'x_sample': nrm((DEC_BATCH, DEC_SEQ, D_MODEL), 1.0),
        'state_pool': nrm((N_EVEN, DEC_BATCH, POOL_BUF, A_WIDTH), 1.0),
        'state_ret': nrm((N_ODD, DEC_BATCH, RET_HEADS, RET_DK, RET_DV), 0.1),
        'state_conv': nrm((DEPTH, DEC_BATCH, CONV_W - 1, D_FF), 1.0),
        'w_mix_in': nrm((N_EVEN, D_MODEL, MIX_IN), D_MODEL ** -0.5),
        'w_pool_grp': nrm((N_EVEN, POOL_GROUPS, POOL_GROUP_DIM, POOL_GROUP_DIM), POOL_GROUP_DIM ** -0.5),
        'pool_scale': gain((N_EVEN, A_WIDTH)),
        'w_spatial': nrm((N_EVEN, SGU_HEADS, SGU_CHUNK, SGU_CHUNK), SGU_CHUNK ** -0.5),
        'b_spatial': gain((N_EVEN, SGU_HEADS, SGU_CHUNK)),
        'sgu_norm_g': gain((N_EVEN, B_WIDTH)),
        'sgu_norm_b': nrm((N_EVEN, B_WIDTH), 0.02),
        'w_mix_out': nrm((N_EVEN, MIX_OUT, D_MODEL), MIX_OUT ** -0.5),
        'w_q': nrm((N_ODD, D_MODEL, RET_HEADS * RET_DK), D_MODEL ** -0.5),
        'w_k': nrm((N_ODD, D_MODEL, RET_HEADS * RET_DK), D_MODEL ** -0.5),
        'w_v': nrm((N_ODD, D_MODEL, RET_HEADS * RET_DV), D_MODEL ** -0.5),
        'w_g': nrm((N_ODD, D_MODEL, RET_HEADS * RET_DV), D_MODEL ** -0.5),
        'ret_norm_g': gain((N_ODD, RET_HEADS, RET_DV)),
        'w_ret_out': nrm((N_ODD, RET_HEADS * RET_DV, D_MODEL), (RET_HEADS * RET_DV) ** -0.5),
        'norm_mix_pre': gain((DEPTH, D_MODEL)),
        'norm_mix_post': gain((DEPTH, D_MODEL)),
        'norm_ffn_pre': gain((DEPTH, D_MODEL)),
        'norm_ffn_post': gain((DEPTH, D_MODEL)),
        'w_ffn_gate': nrm((DEPTH, D_MODEL, D_FF), D_MODEL ** -0.5),
        'w_ffn_up': nrm((DEPTH, D_MODEL, D_FF), D_MODEL ** -0.5),
        'w_dconv': nrm((DEPTH, CONV_W, D_FF), CONV_W ** -0.5),
        'b_dconv': nrm((DEPTH, D_FF), 0.02),
        'w_ffn_down': nrm((DEPTH, D_FF, D_MODEL), D_FF ** -0.5),
    }


def reference(x_prompt, x_sample, state_pool, state_ret, state_conv,
              w_mix_in, w_pool_grp, pool_scale, w_spatial, b_spatial, sgu_norm_g, sgu_norm_b, w_mix_out,
              w_q, w_k, w_v, w_g, ret_norm_g, w_ret_out,
              norm_mix_pre, norm_mix_post, norm_ffn_pre, norm_ffn_post,
              w_ffn_gate, w_ffn_up, w_dconv, b_dconv, w_ffn_down):

    def run(x, pos0, pool_s, ret_s, conv_s):
        new_pool, new_v, new_ret, new_conv = [], [], [], []
        for l in range(DEPTH):
            h = rms_norm(x, norm_mix_pre[l])
            if l % 2 == 0:
                i = l // 2
                m, p_new, v_new = pool_sgu_mixer(h, pool_s[i], pos0, w_mix_in[i], w_pool_grp[i], pool_scale[i],
                                                 w_spatial[i], b_spatial[i], sgu_norm_g[i], sgu_norm_b[i],
                                                 w_mix_out[i])
                new_pool.append(p_new)
                new_v.append(v_new)
            else:
                j = l // 2
                m, s_new = retention_mixer(h, ret_s[j], pos0, w_q[j], w_k[j], w_v[j], w_g[j],
                                           ret_norm_g[j], w_ret_out[j])
                new_ret.append(s_new)
            x = x + rms_norm(m, norm_mix_post[l])
            h = rms_norm(x, norm_ffn_pre[l])
            f, c_new = conv_ffn(h, conv_s[l], w_ffn_gate[l], w_ffn_up[l], w_dconv[l], b_dconv[l], w_ffn_down[l])
            new_conv.append(c_new)
            x = x + rms_norm(f, norm_ffn_post[l])
        return x, jnp.stack(new_pool), jnp.stack(new_v), jnp.stack(new_ret), jnp.stack(new_conv)

    bp = x_prompt.shape[0]
    pool0 = jnp.zeros((N_EVEN, bp, POOL_BUF, A_WIDTH), x_prompt.dtype)
    ret0 = jnp.zeros((N_ODD, bp, RET_HEADS, RET_DK, RET_DV), state_ret.dtype)
    conv0 = jnp.zeros((DEPTH, bp, CONV_W - 1, D_FF), x_prompt.dtype)
    y_prompt, pool_prompt, _, ret_prompt, conv_prompt = run(x_prompt, 0, pool0, ret0, conv0)
    y_sample, pool_sample, chunk_v_sample, ret_sample, conv_sample = run(
        x_sample, PAST_LEN, state_pool, state_ret, state_conv)
    return (y_prompt, y_sample, pool_prompt, pool_sample, chunk_v_sample,
            ret_prompt, ret_sample, conv_prompt, conv_sample)
```

```python
import functools

import jax
import jax.numpy as jnp
from jax import lax
from jax.experimental import pallas as pl
from jax.experimental.pallas import tpu as pltpu

F32 = jnp.float32
BF16 = jnp.bfloat16

PAST_LEN = 16384
POOL_WINDOWS = (2, 4, 8, 16)
POOL_BUF = max(POOL_WINDOWS) - 1
SGU_HEADS = 4
SGU_CHUNK = 128
RET_HEADS = 8
RET_CHUNK = 128
ROPE_BASE = 10000.0
CONV_W = 3
EPS = 1e-6

V7X_VMEM_BYTES = 64 * 1024 * 1024
VMEM_LIMIT_BYTES = V7X_VMEM_BYTES - 8 * 1024 * 1024
SUBLANES = 8
LANES = 128
BF16_SUBLANES = 16

TM = 512
TM_OUT = 256
TM_DOWN = 256


def _cparams(n_axes):
    return pltpu.CompilerParams(
        dimension_semantics=("arbitrary",) * n_axes, vmem_limit_bytes=VMEM_LIMIT_BYTES)


def _resident(shape):
    zeros = (0,) * len(shape)
    return pl.BlockSpec(shape, lambda *_: zeros, pipeline_mode=pl.Buffered(1))


def _rms(x, g):
    return x * lax.rsqrt(jnp.mean(x * x, axis=-1, keepdims=True) + EPS) * g


def _layer_norm(x, g):
    mu = jnp.mean(x, axis=-1, keepdims=True)
    xc = x - mu
    return xc * lax.rsqrt(jnp.mean(xc * xc, axis=-1, keepdims=True) + EPS) * g


def _norm_kernel(x_ref, g_ref, h_ref):
    h_ref[...] = _rms(x_ref[...], g_ref[...]).astype(BF16)


def _norm(x, g):
    m, d = x.shape
    return pl.pallas_call(
        _norm_kernel,
        grid=(m // TM,),
        in_specs=[pl.BlockSpec((TM, d), lambda i: (i, 0)), _resident((1, d))],
        out_specs=pl.BlockSpec((TM, d), lambda i: (i, 0)),
        out_shape=jax.ShapeDtypeStruct((m, d), BF16),
        compiler_params=_cparams(1),
        name="norm",
    )(x, g.reshape(1, d))


def _proj_kernel(h_ref, w_ref, *rest, epilogue, scale):
    z = jnp.dot(h_ref[...], w_ref[...], preferred_element_type=F32)
    if epilogue == "none":
        (o_ref,) = rest
        o_ref[...] = z.astype(o_ref.dtype)
    elif epilogue == "gelu":
        (o_ref,) = rest
        o_ref[...] = jax.nn.gelu(z, approximate=True).astype(o_ref.dtype)
    elif epilogue == "silu":
        (o_ref,) = rest
        o_ref[...] = jax.nn.silu(z).astype(o_ref.dtype)
    elif epilogue == "gelu_ln":
        g_ref, b_ref, o_ref = rest
        v = jax.nn.gelu(z, approximate=True)
        o_ref[...] = (_layer_norm(v, g_ref[...]) + b_ref[...]).astype(o_ref.dtype)
    elif epilogue == "rotary":
        cos_ref, sin_ref, o_ref = rest
        c = cos_ref[...]
        s = sin_ref[...]
        half = c.shape[-1]
        for hd in range(z.shape[-1] // (2 * half)):
            lo = hd * 2 * half
            x1 = z[:, lo:lo + half]
            x2 = z[:, lo + half:lo + 2 * half]
            o_ref[:, lo:lo + half] = ((x1 * c - x2 * s) * scale).astype(o_ref.dtype)
            o_ref[:, lo + half:lo + 2 * half] = ((x1 * s + x2 * c) * scale).astype(o_ref.dtype)
    else:
        raise ValueError(epilogue)


def _proj(h, w, *, tn, out_dtype, epilogue, extras=(), extra_specs=(), scale=1.0, name):
    m, k = h.shape
    n = w.shape[1]
    assert m % TM == 0 and n % tn == 0
    return pl.pallas_call(
        functools.partial(_proj_kernel, epilogue=epilogue, scale=scale),
        grid=(n // tn, m // TM),
        in_specs=[pl.BlockSpec((TM, k), lambda j, i: (i, 0)),
                  pl.BlockSpec((k, tn), lambda j, i: (0, j)),
                  *extra_specs],
        out_specs=pl.BlockSpec((TM, tn), lambda j, i: (i, j)),
        out_shape=jax.ShapeDtypeStruct((m, n), out_dtype),
        compiler_params=_cparams(2),
        name=name,
    )(h, w, *extras)


def _out_kernel(lhs_ref, w_ref, x_ref, gpost_ref, gnext_ref, xo_ref, *maybe_h_ref):
    y = jnp.dot(lhs_ref[...], w_ref[...], preferred_element_type=F32)
    x_new = x_ref[...] + _rms(y, gpost_ref[...])
    xo_ref[...] = x_new
    if maybe_h_ref:
        maybe_h_ref[0][...] = _rms(x_new, gnext_ref[...]).astype(BF16)


def _out_proj(lhs, w, x, g_post, g_next, *, tm, name):
    m, k = lhs.shape
    d = w.shape[1]
    want_h = g_next is not None
    if g_next is None:
        g_next = g_post
    row = lambda i: (i, 0)
    out_specs = [pl.BlockSpec((tm, d), row)]
    out_shape = [jax.ShapeDtypeStruct((m, d), F32)]
    if want_h:
        out_specs.append(pl.BlockSpec((tm, d), row))
        out_shape.append(jax.ShapeDtypeStruct((m, d), BF16))
    res = pl.pallas_call(
        _out_kernel,
        grid=(m // tm,),
        in_specs=[pl.BlockSpec((tm, k), row), _resident((k, d)), pl.BlockSpec((tm, d), row),
                  _resident((1, d)), _resident((1, d))],
        out_specs=out_specs,
        out_shape=out_shape,
        compiler_params=_cparams(1),
        name=name,
    )(lhs, w, x, g_post.reshape(1, d), g_next.reshape(1, d))
    return (res[0], res[1]) if want_h else (res[0], None)


def _mix_group(i_local, a_ref, u_ref, vn_ref, wgrp_ref, pscale_ref, wmix_ref, bias_ref, o_ref, ext_ref,
               inv_cnt, halo_rows, shift, chunk):
    tm, a_width = a_ref.shape
    gdim = a_width // len(POOL_WINDOWS)
    ext_ref[halo_rows:halo_rows + tm, :] = a_ref[...]
    for gi, w in enumerate(POOL_WINDOWS):
        c0, c1 = gi * gdim, (gi + 1) * gdim
        s = ext_ref[halo_rows:halo_rows + tm, c0:c1]
        for j in range(1, w):
            s = s + ext_ref[halo_rows - j * shift:halo_rows - j * shift + tm, c0:c1]
        d = (s * inv_cnt(gi) - a_ref[:, c0:c1]).astype(BF16)
        z = jnp.dot(d, wgrp_ref[gi], preferred_element_type=F32)
        o_ref[:, c0:c1] = (z * pscale_ref[:, c0:c1]).astype(o_ref.dtype)
    hdim = vn_ref.shape[1] // SGU_HEADS
    for c in range(tm // chunk):
        r0, r1 = c * chunk, (c + 1) * chunk
        for hd in range(SGU_HEADS):
            c0, c1 = hd * hdim, (hd + 1) * hdim
            mixed = jnp.dot(wmix_ref[hd], vn_ref[r0:r1, c0:c1].astype(BF16),
                            preferred_element_type=F32) + bias_ref[hd]
            o_ref[r0:r1, a_width + c0:a_width + c1] = (
                u_ref[r0:r1, c0:c1].astype(F32) * mixed).astype(o_ref.dtype)


def _mix_kernel(a_ref, u_ref, vn_ref, invc_ref, halo_s_ref, wgrp_ref, pscale_ref,
                wmix_p_ref, bias_p_ref, wmix_s_ref, bias_s_ref, o_ref, ext_ref,
                *, n_prompt_tiles, tiles_per_seq, dec_batch):
    i = pl.program_id(0)
    tm = a_ref.shape[0]
    gl = LANES
    halo_p = 2 * SUBLANES

    @pl.when(i < n_prompt_tiles)
    def _prompt():
        @pl.when(i % tiles_per_seq == 0)
        def _():
            ext_ref[0:halo_p, :] = jnp.zeros((halo_p, ext_ref.shape[1]), F32)

        def inv_cnt(gi):
            blk = invc_ref[:, gi * gl:(gi + 1) * gl]
            return jnp.concatenate([blk, blk], axis=1)

        _mix_group(i, a_ref, u_ref, vn_ref, wgrp_ref, pscale_ref, wmix_p_ref, bias_p_ref, o_ref, ext_ref,
                   inv_cnt, halo_p, 1, SGU_CHUNK)
        ext_ref[0:halo_p, :] = ext_ref[tm:tm + halo_p, :]

    @pl.when(i >= n_prompt_tiles)
    def _sample():
        halo_s = POOL_BUF * dec_batch
        ext_ref[0:halo_s, :] = halo_s_ref[...]
        _mix_group(i, a_ref, u_ref, vn_ref, wgrp_ref, pscale_ref, wmix_s_ref, bias_s_ref, o_ref, ext_ref,
                   lambda gi: 1.0 / POOL_WINDOWS[gi], halo_s, dec_batch, tm)


def _mix(a, u, vn, invc, halo_s, wgrp, pscale, wmix_p, bias_p, wmix_s, bias_s, *, n_prompt_tiles, tiles_per_seq,
         dec_batch):
    m, a_width = a.shape
    b_width = u.shape[1]
    row = lambda i: (i, 0)
    ext_rows = max(2 * SUBLANES, POOL_BUF * dec_batch) + TM
    return pl.pallas_call(
        functools.partial(_mix_kernel, n_prompt_tiles=n_prompt_tiles, tiles_per_seq=tiles_per_seq,
                          dec_batch=dec_batch),
        grid=(m // TM,),
        in_specs=[pl.BlockSpec((TM, a_width), row), pl.BlockSpec((TM, b_width), row),
                  pl.BlockSpec((TM, b_width), row),
                  pl.BlockSpec((TM, invc.shape[1]), lambda i: (jnp.minimum(i, n_prompt_tiles - 1) % tiles_per_seq, 0)),
                  _resident(halo_s.shape), _resident(wgrp.shape), _resident(pscale.shape),
                  _resident(wmix_p.shape), _resident(bias_p.shape), _resident(wmix_s.shape),
                  _resident(bias_s.shape)],
        out_specs=pl.BlockSpec((TM, a_width + b_width), row),
        out_shape=jax.ShapeDtypeStruct((m, a_width + b_width), BF16),
        scratch_shapes=[pltpu.VMEM((ext_rows, a_width), F32)],
        compiler_params=_cparams(1),
        name="mix",
    )(a, u, vn, invc, halo_s, wgrp, pscale, wmix_p, bias_p, wmix_s, bias_s)


def _ffn_in_kernel(h_ref, wg_ref, wu_ref, wc_ref, bc_ref, cstate_ref, act_ref, tail_p_ref, tail_s_ref, ext_ref,
                   *, n_prompt_tiles, tiles_per_seq, dec_batch):
    i = pl.program_id(1)
    tm = h_ref.shape[0]
    h = h_ref[...]
    gate = jnp.dot(h, wg_ref[...], preferred_element_type=F32)
    up = jnp.dot(h, wu_ref[...], preferred_element_type=F32)
    w0, w1, w2 = wc_ref[0:1, :], wc_ref[1:2, :], wc_ref[2:3, :]
    halo_p = SUBLANES

    def finish(prev2, prev1):
        conv = bc_ref[...] + prev2 * w0
        conv = conv + prev1 * w1
        conv = conv + gate * w2
        act_ref[...] = (jax.nn.gelu(conv, approximate=True) * up).astype(act_ref.dtype)

    @pl.when(i < n_prompt_tiles)
    def _prompt():
        @pl.when(i % tiles_per_seq == 0)
        def _():
            ext_ref[0:halo_p, :] = jnp.zeros((halo_p, ext_ref.shape[1]), F32)

        ext_ref[halo_p:halo_p + tm, :] = gate
        finish(ext_ref[halo_p - 2:halo_p - 2 + tm, :], ext_ref[halo_p - 1:halo_p - 1 + tm, :])
        tail = gate[tm - halo_p:tm, :]
        tail_p_ref[...] = tail
        ext_ref[0:halo_p, :] = tail

    @pl.when(i >= n_prompt_tiles)
    def _sample():
        halo_s = (CONV_W - 1) * dec_batch
        ext_ref[0:halo_s, :] = cstate_ref[...]
        ext_ref[halo_s:halo_s + tm, :] = gate
        finish(ext_ref[0:tm, :], ext_ref[dec_batch:dec_batch + tm, :])
        tail_s_ref[...] = gate[tm - halo_s:tm, :]


def _ffn_in(h, wg, wu, wc, bc, cstate, *, tn, n_prompt_tiles, tiles_per_seq, dec_batch):
    m, k = h.shape
    n = wg.shape[1]
    halo_s = (CONV_W - 1) * dec_batch
    last_p = n_prompt_tiles - 1
    return pl.pallas_call(
        functools.partial(_ffn_in_kernel, n_prompt_tiles=n_prompt_tiles, tiles_per_seq=tiles_per_seq,
                          dec_batch=dec_batch),
        grid=(n // tn, m // TM),
        in_specs=[pl.BlockSpec((TM, k), lambda j, i: (i, 0)),
                  pl.BlockSpec((k, tn), lambda j, i: (0, j)),
                  pl.BlockSpec((k, tn), lambda j, i: (0, j)),
                  pl.BlockSpec((CONV_W, tn), lambda j, i: (0, j)),
                  pl.BlockSpec((1, tn), lambda j, i: (0, j)),
                  pl.BlockSpec((halo_s, tn), lambda j, i: (0, j))],
        out_specs=[pl.BlockSpec((TM, tn), lambda j, i: (i, j)),
                   pl.BlockSpec((SUBLANES, tn), lambda j, i: (jnp.minimum(i, last_p), j)),
                   pl.BlockSpec((halo_s, tn), lambda j, i: (0, j))],
        out_shape=[jax.ShapeDtypeStruct((m, n), BF16),
                   jax.ShapeDtypeStruct((n_prompt_tiles * SUBLANES, n), F32),
                   jax.ShapeDtypeStruct((halo_s, n), F32)],
        scratch_shapes=[pltpu.VMEM((max(SUBLANES, halo_s) + TM, tn), F32)],
        compiler_params=_cparams(2),
        name="ffn_in",
    )(h, wg, wu, wc, bc, cstate)


def _ret_kernel(q_ref, k_ref, v_ref, g_ref, s0_ref, dmask_ref, xi_ref, zeta_ref, gc_ref, gn_ref, o_ref, s_ref):
    @pl.when(pl.program_id(1) == 0)
    def _():
        s_ref[...] = s0_ref[...]

    dk = q_ref.shape[1] // RET_HEADS
    dv = v_ref.shape[1] // RET_HEADS
    for hd in range(RET_HEADS):
        qh = q_ref[:, hd * dk:(hd + 1) * dk]
        kh = k_ref[:, hd * dk:(hd + 1) * dk]
        vh = v_ref[:, hd * dv:(hd + 1) * dv]
        state = s_ref[0, hd]
        sc = lax.dot_general(qh, kh, (((1,), (1,)), ((), ())), preferred_element_type=F32) * dmask_ref[hd]
        o = jnp.dot(sc.astype(BF16), vh, preferred_element_type=F32)
        o = o + jnp.dot(qh, state.astype(BF16), preferred_element_type=F32) * xi_ref[hd]
        kz = (kh.astype(F32) * zeta_ref[hd]).astype(BF16)
        s_ref[0, hd] = gc_ref[hd] * state + lax.dot_general(
            kz, vh, (((0,), (0,)), ((), ())), preferred_element_type=F32)
        on = _layer_norm(o, gn_ref[hd])
        o_ref[:, hd * dv:(hd + 1) * dv] = (g_ref[:, hd * dv:(hd + 1) * dv].astype(F32) * on).astype(o_ref.dtype)


def _retention(q, k, v, g, s0, dmask, xi, zeta, gc, gn, *, n_seq, n_chunk, chunk, out_rows, name):
    dq, dvv = q.shape[1], v.shape[1]
    blk = lambda b, c: (b * n_chunk + c, 0)
    st = lambda b, c: (b, 0, 0, 0)
    return pl.pallas_call(
        _ret_kernel,
        grid=(n_seq, n_chunk),
        in_specs=[pl.BlockSpec((chunk, dq), blk), pl.BlockSpec((chunk, dq), blk),
                  pl.BlockSpec((chunk, dvv), blk), pl.BlockSpec((chunk, dvv), blk),
                  pl.BlockSpec((1,) + s0.shape[1:], st),
                  _resident(dmask.shape), _resident(xi.shape), _resident(zeta.shape), _resident(gc.shape),
                  _resident(gn.shape)],
        out_specs=[pl.BlockSpec((chunk, dvv), blk), pl.BlockSpec((1,) + s0.shape[1:], st)],
        out_shape=[jax.ShapeDtypeStruct((out_rows, dvv), BF16), jax.ShapeDtypeStruct(s0.shape, F32)],
        compiler_params=_cparams(2),
        name=name,
    )(q, k, v, g, s0, dmask, xi, zeta, gc, gn)


def _decay_tables(c_true, c_pad, dk, dv):
    lg = jnp.log1p(-jnp.exp2(-5.0 - jnp.arange(RET_HEADS, dtype=F32)))
    idx = jnp.arange(c_true, dtype=F32)
    diff = idx[:, None] - idx[None, :]
    dmask = jnp.where(diff >= 0, jnp.exp(lg[:, None, None] * jnp.maximum(diff, 0.0)), 0.0)
    xi = jnp.exp(lg[:, None] * (idx + 1.0))
    zeta = jnp.exp(lg[:, None] * (c_true - 1.0 - idx))
    g_c = jnp.exp(lg * c_true)
    pad = c_pad - c_true
    dmask = jnp.pad(dmask, ((0, 0), (0, pad), (0, pad)))
    xi = jnp.pad(xi, ((0, 0), (0, pad)))
    zeta = jnp.pad(zeta, ((0, 0), (0, pad)))
    return (dmask,
            jnp.broadcast_to(xi[:, :, None], (RET_HEADS, c_pad, dv)),
            jnp.broadcast_to(zeta[:, :, None], (RET_HEADS, c_pad, dk)),
            jnp.broadcast_to(g_c[:, None, None], (RET_HEADS, 1, dv)))


def kernel(x_prompt, x_sample, state_pool, state_ret, state_conv, w_mix_in, w_pool_grp, pool_scale, w_spatial,
           b_spatial, sgu_norm_g, sgu_norm_b, w_mix_out, w_q, w_k, w_v, w_g, ret_norm_g, w_ret_out, norm_mix_pre,
           norm_mix_post, norm_ffn_pre, norm_ffn_post, w_ffn_gate, w_ffn_up, w_dconv, b_dconv, w_ffn_down):
    bp, seq, d = x_prompt.shape
    bs, dec_seq, _ = x_sample.shape
    depth = norm_mix_pre.shape[0]
    a_width = w_pool_grp.shape[1] * w_pool_grp.shape[2]
    b_width = sgu_norm_g.shape[1]
    dk = w_q.shape[2] // RET_HEADS
    dv = w_v.shape[2] // RET_HEADS
    d_ff = w_ffn_gate.shape[2]
    m_p, m_s = bp * seq, bs * dec_seq
    m = m_p + m_s
    assert m_s == TM and seq % TM == 0 and seq % RET_CHUNK == 0 and seq >= POOL_BUF
    assert CONV_W - 1 <= dec_seq < POOL_BUF and dec_seq <= SGU_CHUNK and dec_seq <= BF16_SUBLANES
    n_prompt_tiles = m_p // TM
    tiles_per_seq = seq // TM
    geom = dict(n_prompt_tiles=n_prompt_tiles, tiles_per_seq=tiles_per_seq, dec_batch=bs)

    def to_rows(t):
        return t.transpose(1, 0, 2).reshape(dec_seq * bs, t.shape[-1])

    def from_rows(r, steps):
        return r.reshape(steps, bs, r.shape[-1]).transpose(1, 0, 2)

    x = jnp.concatenate([x_prompt.reshape(m_p, d), to_rows(x_sample)], axis=0)

    half = dk // 2
    inv = ROPE_BASE ** (-jnp.arange(half, dtype=F32) / half)
    pos_p = jnp.arange(seq).astype(F32)
    pos_s = (PAST_LEN + jnp.arange(dec_seq)).astype(F32)
    ang = jnp.concatenate([jnp.tile(pos_p[:, None] * inv[None, :], (bp, 1)),
                           jnp.repeat(pos_s[:, None] * inv[None, :], bs, axis=0)], axis=0)
    cos_t, sin_t = jnp.cos(ang), jnp.sin(ang)
    rot_specs = (pl.BlockSpec((TM, half), lambda j, i: (i, 0)),) * 2

    pos = jnp.arange(seq)
    invc = jnp.concatenate(
        [jnp.broadcast_to((1.0 / jnp.minimum(pos + 1, w).astype(F32))[:, None], (seq, LANES)) for w in POOL_WINDOWS],
        axis=1)

    tril_p = jnp.tril(jnp.ones((SGU_CHUNK, SGU_CHUNK), dtype=bool))
    tril_s = jnp.tril(jnp.ones((dec_seq, dec_seq), dtype=bool))
    eye_b = jnp.eye(bs, dtype=F32)
    hdim = b_width // SGU_HEADS

    dec_pad = BF16_SUBLANES
    tabs_p = _decay_tables(RET_CHUNK, RET_CHUNK, dk, dv)
    tabs_s = _decay_tables(dec_seq, dec_pad, dk, dv)

    def pad_steps(r):
        t = from_rows(r, dec_seq)
        return jnp.pad(t, ((0, 0), (0, dec_pad - dec_seq), (0, 0))).reshape(bs * dec_pad, r.shape[-1])

    pool_p, pool_s, vn_s, ret_p, ret_s, conv_p, conv_s = [], [], [], [], [], [], []
    h = _norm(x, norm_mix_pre[0])
    for l in range(depth):
        if l % 2 == 0:
            e = l // 2
            w_in = w_mix_in[e].astype(BF16)
            a = _proj(h, w_in[:, :a_width], tn=a_width, out_dtype=F32, epilogue="none", name="proj_a")
            u = _proj(h, w_in[:, a_width:a_width + b_width], tn=b_width, out_dtype=BF16, epilogue="gelu",
                      name="proj_u")
            vn = _proj(h, w_in[:, a_width + b_width:], tn=b_width, out_dtype=F32, epilogue="gelu_ln",
                       extras=(sgu_norm_g[e].reshape(1, b_width), sgu_norm_b[e].reshape(1, b_width)),
                       extra_specs=(pl.BlockSpec((1, b_width), lambda j, i: (0, 0)),) * 2, name="proj_v")
            ws_p = jnp.where(tril_p, w_spatial[e][:, :SGU_CHUNK, :SGU_CHUNK], 0.0)
            ws_s = jnp.where(tril_s, w_spatial[e][:, :dec_seq, :dec_seq], 0.0)
            wmix_s = jnp.einsum("hij,bc->hibjc", ws_s, eye_b).reshape(SGU_HEADS, m_s, m_s)
            bias_p = jnp.broadcast_to(b_spatial[e][:, :SGU_CHUNK, None], (SGU_HEADS, SGU_CHUNK, hdim))
            bias_s = jnp.broadcast_to(b_spatial[e][:, :dec_seq, None, None],
                                      (SGU_HEADS, dec_seq, bs, hdim)).reshape(SGU_HEADS, m_s, hdim)
            halo_s = state_pool[e].transpose(1, 0, 2).reshape(POOL_BUF * bs, a_width)
            mixed = _mix(a, u, vn, invc, halo_s, w_pool_grp[e].astype(BF16), pool_scale[e].reshape(1, a_width),
                         ws_p.astype(BF16), bias_p, wmix_s.astype(BF16), bias_s, **geom)
            w_o = w_mix_out[e].astype(BF16)
            pool_p.append(a[:m_p].reshape(bp, seq, a_width)[:, seq - POOL_BUF:])
            pool_s.append(jnp.concatenate([state_pool[e][:, dec_seq:], from_rows(a[m_p:], dec_seq)], axis=1))
            vn_s.append(from_rows(vn[m_p:], dec_seq))
        else:
            r = l // 2
            q = _proj(h, w_q[r].astype(BF16), tn=1024, out_dtype=BF16, epilogue="rotary", extras=(cos_t, sin_t),
                      extra_specs=rot_specs, name="proj_q")
            k = _proj(h, w_k[r].astype(BF16), tn=1024, out_dtype=BF16, epilogue="rotary", extras=(cos_t, sin_t),
                      extra_specs=rot_specs, scale=dk ** -0.5, name="proj_k")
            v = _proj(h, w_v[r].astype(BF16), tn=1024, out_dtype=BF16, epilogue="none", name="proj_v_ret")
            g = _proj(h, w_g[r].astype(BF16), tn=1024, out_dtype=BF16, epilogue="silu", name="proj_g")
            gn = ret_norm_g[r].reshape(RET_HEADS, 1, dv)
            s0_p = jnp.zeros((bp,) + state_ret.shape[2:], state_ret.dtype)
            gated, s_p = _retention(q, k, v, g, s0_p, *tabs_p, gn, n_seq=bp, n_chunk=seq // RET_CHUNK,
                                    chunk=RET_CHUNK, out_rows=m, name="ret_prompt")
            gated_s, s_s = _retention(pad_steps(q[m_p:]), pad_steps(k[m_p:]), pad_steps(v[m_p:]), pad_steps(g[m_p:]),
                                      state_ret[r], *tabs_s, gn, n_seq=bs, n_chunk=1, chunk=dec_pad,
                                      out_rows=bs * dec_pad, name="ret_sample")
            gated_s = to_rows(gated_s.reshape(bs, dec_pad, -1)[:, :dec_seq])
            mixed = lax.dynamic_update_slice(gated, gated_s, (m_p, 0))
            w_o = w_ret_out[r].astype(BF16)
            ret_p.append(s_p)
            ret_s.append(s_s)
        x, h = _out_proj(mixed, w_o, x, norm_mix_post[l], norm_ffn_pre[l], tm=TM_OUT, name="mix_out")
        cstate = state_conv[l].transpose(1, 0, 2).reshape((CONV_W - 1) * bs, d_ff)
        act, tail_p, tail_s = _ffn_in(h, w_ffn_gate[l].astype(BF16), w_ffn_up[l].astype(BF16), w_dconv[l],
                                      b_dconv[l].reshape(1, d_ff), cstate, tn=512, **geom)
        conv_p.append(tail_p.reshape(bp, tiles_per_seq, SUBLANES, d_ff)[:, -1, SUBLANES - (CONV_W - 1):])
        conv_s.append(from_rows(tail_s, CONV_W - 1))
        g_next = norm_mix_pre[l + 1] if l + 1 < depth else None
        x, h = _out_proj(act, w_ffn_down[l].astype(BF16), x, norm_ffn_post[l], g_next, tm=TM_DOWN, name="ffn_out")

    y_prompt = x[:m_p].reshape(bp, seq, d)
    y_sample = from_rows(x[m_p:], dec_seq)
    return (y_prompt, y_sample, jnp.stack(pool_p), jnp.stack(pool_s), jnp.stack(vn_s),
            jnp.stack(ret_p), jnp.stack(ret_s), jnp.stack(conv_p), jnp.stack(conv_s))
```

```python
import functools

import jax
import jax.numpy as jnp
from jax import lax
from jax.experimental import pallas as pl
from jax.experimental.pallas import tpu as pltpu

F32 = jnp.float32
BF16 = jnp.bfloat16

PAST_LEN = 16384
POOL_WINDOWS = (2, 4, 8, 16)
POOL_BUF = max(POOL_WINDOWS) - 1
SGU_HEADS = 4
SGU_CHUNK = 128
RET_HEADS = 8
RET_CHUNK = 128
ROPE_BASE = 10000.0
CONV_W = 3
EPS = 1e-6

V7X_VMEM_BYTES = 64 * 1024 * 1024
VMEM_LIMIT_BYTES = V7X_VMEM_BYTES - 8 * 1024 * 1024
SUBLANES = 8
LANES = 128
BF16_SUBLANES = 16

TM = 512
TM_PROJ = 1088
TM_FFN = 1024
TN_FFN = 512
TM_OUT = 256
TM_DOWN = 256


def _cparams(n_axes):
    return pltpu.CompilerParams(
        dimension_semantics=("arbitrary",) * n_axes, vmem_limit_bytes=VMEM_LIMIT_BYTES)


def _resident(shape):
    zeros = (0,) * len(shape)
    return pl.BlockSpec(shape, lambda *_: zeros, pipeline_mode=pl.Buffered(1))


def _rms(x, g):
    return x * lax.rsqrt(jnp.mean(x * x, axis=-1, keepdims=True) + EPS) * g


def _layer_norm(x, g):
    mu = jnp.mean(x, axis=-1, keepdims=True)
    xc = x - mu
    return xc * lax.rsqrt(jnp.mean(xc * xc, axis=-1, keepdims=True) + EPS) * g


def _norm_kernel(x_ref, g_ref, h_ref):
    h_ref[...] = _rms(x_ref[...], g_ref[...]).astype(BF16)


def _norm(x, g):
    m, d = x.shape
    return pl.pallas_call(
        _norm_kernel,
        grid=(m // TM,),
        in_specs=[pl.BlockSpec((TM, d), lambda i: (i, 0)), _resident((1, d))],
        out_specs=pl.BlockSpec((TM, d), lambda i: (i, 0)),
        out_shape=jax.ShapeDtypeStruct((m, d), BF16),
        compiler_params=_cparams(1),
        name="norm",
    )(x, g.reshape(1, d))


def _proj_kernel(h_ref, w_ref, *rest, epilogue, scale):
    *extras, o_ref, wb_ref = rest

    @pl.when(pl.program_id(1) == 0)
    def _():
        wb_ref[...] = w_ref[...].astype(BF16)

    z = jnp.dot(h_ref[...], wb_ref[...], preferred_element_type=F32)
    if epilogue == "none":
        o_ref[...] = z.astype(o_ref.dtype)
    elif epilogue == "gelu":
        o_ref[...] = jax.nn.gelu(z, approximate=True).astype(o_ref.dtype)
    elif epilogue == "silu":
        o_ref[...] = jax.nn.silu(z).astype(o_ref.dtype)
    elif epilogue == "gelu_ln":
        g_ref, b_ref = extras
        v = jax.nn.gelu(z, approximate=True)
        o_ref[...] = (_layer_norm(v, g_ref[...]) + b_ref[...]).astype(o_ref.dtype)
    elif epilogue == "rotary":
        cos_ref, sin_ref = extras
        c = cos_ref[...]
        s = sin_ref[...]
        half = c.shape[-1]
        for hd in range(z.shape[-1] // (2 * half)):
            lo = hd * 2 * half
            x1 = z[:, lo:lo + half]
            x2 = z[:, lo + half:lo + 2 * half]
            o_ref[:, lo:lo + half] = ((x1 * c - x2 * s) * scale).astype(o_ref.dtype)
            o_ref[:, lo + half:lo + 2 * half] = ((x1 * s + x2 * c) * scale).astype(o_ref.dtype)
    else:
        raise ValueError(epilogue)


def _proj(h, w, *, layer, col0, n_out, tn, out_dtype, epilogue, extras=(), extra_specs=(), scale=1.0, name):
    m, k = h.shape
    assert m % TM_PROJ == 0 and n_out % tn == 0
    return pl.pallas_call(
        functools.partial(_proj_kernel, epilogue=epilogue, scale=scale),
        grid=(n_out // tn, m // TM_PROJ),
        in_specs=[pl.BlockSpec((TM_PROJ, k), lambda j, i: (i, 0)),
                  pl.BlockSpec((None, k, tn), lambda j, i: (layer, 0, col0 + j)),
                  *extra_specs],
        out_specs=pl.BlockSpec((TM_PROJ, tn), lambda j, i: (i, j)),
        out_shape=jax.ShapeDtypeStruct((m, n_out), out_dtype),
        scratch_shapes=[pltpu.VMEM((k, tn), BF16)],
        compiler_params=_cparams(2),
        name=name,
    )(h, w, *extras)


def _out_kernel(lhs_ref, w_ref, x_ref, gpost_ref, gnext_ref, xo_ref, *maybe_h_ref):
    y = jnp.dot(lhs_ref[...], w_ref[...], preferred_element_type=F32)
    x_new = x_ref[...] + _rms(y, gpost_ref[...])
    xo_ref[...] = x_new
    if maybe_h_ref:
        maybe_h_ref[0][...] = _rms(x_new, gnext_ref[...]).astype(BF16)


def _out_proj(lhs, w, x, g_post, g_next, *, tm, name):
    m, k = lhs.shape
    d = w.shape[1]
    want_h = g_next is not None
    if g_next is None:
        g_next = g_post
    row = lambda i: (i, 0)
    out_specs = [pl.BlockSpec((tm, d), row)]
    out_shape = [jax.ShapeDtypeStruct((m, d), F32)]
    if want_h:
        out_specs.append(pl.BlockSpec((tm, d), row))
        out_shape.append(jax.ShapeDtypeStruct((m, d), BF16))
    res = pl.pallas_call(
        _out_kernel,
        grid=(m // tm,),
        in_specs=[pl.BlockSpec((tm, k), row), _resident((k, d)), pl.BlockSpec((tm, d), row),
                  _resident((1, d)), _resident((1, d))],
        out_specs=out_specs,
        out_shape=out_shape,
        compiler_params=_cparams(1),
        name=name,
    )(lhs, w, x, g_post.reshape(1, d), g_next.reshape(1, d))
    return (res[0], res[1]) if want_h else (res[0], None)


def _mix_group(a_ref, u_ref, vn_ref, wgrp_ref, pscale_ref, wmix_ref, bias_ref, o_ref, ext_ref,
               inv_cnt, halo_rows, shift, chunk):
    tm, a_width = a_ref.shape
    gdim = a_width // len(POOL_WINDOWS)
    ext_ref[halo_rows:halo_rows + tm, :] = a_ref[...]
    for gi, w in enumerate(POOL_WINDOWS):
        c0, c1 = gi * gdim, (gi + 1) * gdim
        s = ext_ref[halo_rows:halo_rows + tm, c0:c1]
        for j in range(1, w):
            s = s + ext_ref[halo_rows - j * shift:halo_rows - j * shift + tm, c0:c1]
        d = (s * inv_cnt(gi) - a_ref[:, c0:c1]).astype(BF16)
        z = jnp.dot(d, wgrp_ref[gi], preferred_element_type=F32)
        o_ref[:, c0:c1] = (z * pscale_ref[:, c0:c1]).astype(o_ref.dtype)
    hdim = vn_ref.shape[1] // SGU_HEADS
    for c in range(tm // chunk):
        r0, r1 = c * chunk, (c + 1) * chunk
        for hd in range(SGU_HEADS):
            c0, c1 = hd * hdim, (hd + 1) * hdim
            mixed = jnp.dot(wmix_ref[hd], vn_ref[r0:r1, c0:c1].astype(BF16),
                            preferred_element_type=F32) + bias_ref[hd]
            o_ref[r0:r1, a_width + c0:a_width + c1] = (
                u_ref[r0:r1, c0:c1].astype(F32) * mixed).astype(o_ref.dtype)


def _mix_kernel(a_ref, u_ref, vn_ref, invc_ref, halo_s_ref, wgrp_ref, pscale_ref,
                wmix_p_ref, bias_p_ref, wmix_s_ref, bias_s_ref, o_ref, ext_ref,
                *, n_prompt_tiles, tiles_per_seq, dec_batch):
    i = pl.program_id(0)
    tm = a_ref.shape[0]
    gl = LANES
    halo_p = 2 * SUBLANES

    @pl.when(i < n_prompt_tiles)
    def _prompt():
        @pl.when(i % tiles_per_seq == 0)
        def _():
            ext_ref[0:halo_p, :] = jnp.zeros((halo_p, ext_ref.shape[1]), F32)

        def inv_cnt(gi):
            blk = invc_ref[:, gi * gl:(gi + 1) * gl]
            return jnp.concatenate([blk, blk], axis=1)

        _mix_group(a_ref, u_ref, vn_ref, wgrp_ref, pscale_ref, wmix_p_ref, bias_p_ref, o_ref, ext_ref,
                   inv_cnt, halo_p, 1, SGU_CHUNK)
        ext_ref[0:halo_p, :] = ext_ref[tm:tm + halo_p, :]

    @pl.when(i >= n_prompt_tiles)
    def _sample():
        halo_s = POOL_BUF * dec_batch
        ext_ref[0:halo_s, :] = halo_s_ref[...]
        _mix_group(a_ref, u_ref, vn_ref, wgrp_ref, pscale_ref, wmix_s_ref, bias_s_ref, o_ref, ext_ref,
                   lambda gi: 1.0 / POOL_WINDOWS[gi], halo_s, dec_batch, tm)


def _mix(a, u, vn, invc, halo_s, wgrp, pscale, wmix_p, bias_p, wmix_s, bias_s, *, n_prompt_tiles, tiles_per_seq,
         dec_batch):
    m, a_width = a.shape
    b_width = u.shape[1]
    row = lambda i: (i, 0)
    ext_rows = max(2 * SUBLANES, POOL_BUF * dec_batch) + TM
    return pl.pallas_call(
        functools.partial(_mix_kernel, n_prompt_tiles=n_prompt_tiles, tiles_per_seq=tiles_per_seq,
                          dec_batch=dec_batch),
        grid=(m // TM,),
        in_specs=[pl.BlockSpec((TM, a_width), row), pl.BlockSpec((TM, b_width), row),
                  pl.BlockSpec((TM, b_width), row),
                  pl.BlockSpec((TM, invc.shape[1]), lambda i: (jnp.minimum(i, n_prompt_tiles - 1) % tiles_per_seq, 0)),
                  _resident(halo_s.shape), _resident(wgrp.shape), _resident(pscale.shape),
                  _resident(wmix_p.shape), _resident(bias_p.shape), _resident(wmix_s.shape),
                  _resident(bias_s.shape)],
        out_specs=pl.BlockSpec((TM, a_width + b_width), row),
        out_shape=jax.ShapeDtypeStruct((m, a_width + b_width), BF16),
        scratch_shapes=[pltpu.VMEM((ext_rows, a_width), F32)],
        compiler_params=_cparams(1),
        name="mix",
    )(a, u, vn, invc, halo_s, wgrp, pscale, wmix_p, bias_p, wmix_s, bias_s)


def _ffn_in_kernel(h_ref, wg_ref, wu_ref, wc_ref, bc_ref, cstate_ref, *rest, shift, tiles_per_seq):
    act_ref, tail_ref, wgb_ref, wub_ref, ext_ref = rest[-5:]
    i = pl.program_id(1)
    tm = h_ref.shape[0]
    halo = cstate_ref.shape[0]

    @pl.when(i == 0)
    def _():
        wgb_ref[...] = wg_ref[...].astype(BF16)
        wub_ref[...] = wu_ref[...].astype(BF16)
        ext_ref[0:halo, :] = cstate_ref[...]

    h = h_ref[...]
    gate = jnp.dot(h, wgb_ref[...], preferred_element_type=F32)
    up = jnp.dot(h, wub_ref[...], preferred_element_type=F32)
    if tiles_per_seq > 1:
        seq_start = (i % tiles_per_seq) == 0
        ext_ref[0:halo, :] = jnp.where(seq_start, cstate_ref[...], ext_ref[0:halo, :])
    ext_ref[halo:halo + tm, :] = gate
    conv = bc_ref[...] + ext_ref[halo - 2 * shift:halo - 2 * shift + tm, :] * wc_ref[0:1, :]
    conv = conv + ext_ref[halo - shift:halo - shift + tm, :] * wc_ref[1:2, :]
    conv = conv + gate * wc_ref[2:3, :]
    act_ref[...] = (jax.nn.gelu(conv, approximate=True) * up).astype(act_ref.dtype)
    tail = gate[tm - halo:tm, :]
    tail_ref[...] = tail
    if tiles_per_seq > 1:
        ext_ref[0:halo, :] = tail


def _ffn_in(h, wg, wu, wc, bc, cstate, act_prev, *, layer, tm, tn, row0, n_tiles, tiles_per_seq, shift):
    m, k = h.shape
    n = wg.shape[2]
    halo = cstate.shape[0]
    wspec = pl.BlockSpec((None, k, tn), lambda j, i: (layer, 0, j))
    in_specs = [pl.BlockSpec((tm, k), lambda j, i: (row0 + i, 0)), wspec, wspec,
                pl.BlockSpec((None, CONV_W, tn), lambda j, i: (layer, 0, j)),
                pl.BlockSpec((None, 1, tn), lambda j, i: (layer, 0, j)),
                pl.BlockSpec((halo, tn), lambda j, i: (0, j))]
    args = [h, wg, wu, wc, bc, cstate]
    aliases = {}
    if act_prev is not None:
        in_specs.append(pl.BlockSpec(memory_space=pl.ANY))
        args.append(act_prev)
        aliases = {len(args) - 1: 0}
    return pl.pallas_call(
        functools.partial(_ffn_in_kernel, shift=shift, tiles_per_seq=tiles_per_seq),
        grid=(n // tn, n_tiles),
        in_specs=in_specs,
        out_specs=[pl.BlockSpec((tm, tn), lambda j, i: (row0 + i, j)),
                   pl.BlockSpec((halo, tn), lambda j, i: (i, j))],
        out_shape=[jax.ShapeDtypeStruct((m, n), BF16),
                   jax.ShapeDtypeStruct((n_tiles * halo, n), F32)],
        scratch_shapes=[pltpu.VMEM((k, tn), BF16), pltpu.VMEM((k, tn), BF16), pltpu.VMEM((halo + tm, tn), F32)],
        input_output_aliases=aliases,
        compiler_params=_cparams(2),
        name="ffn_in_sample" if act_prev is not None else "ffn_in_prompt",
    )(*args)


def _ret_kernel(q_ref, k_ref, v_ref, g_ref, *rest, zero_init):
    s_ref = rest[-1]
    o_ref = rest[-2]
    if zero_init:
        dmask_ref, xi_ref, zeta_ref, gc_ref, gn_ref = rest[:5]
    else:
        s0_ref, dmask_ref, xi_ref, zeta_ref, gc_ref, gn_ref = rest[:6]

    @pl.when(pl.program_id(1) == 0)
    def _():
        s_ref[...] = jnp.zeros(s_ref.shape, F32) if zero_init else s0_ref[...]

    dk = q_ref.shape[1] // RET_HEADS
    dv = v_ref.shape[1] // RET_HEADS
    for hd in range(RET_HEADS):
        qh = q_ref[:, hd * dk:(hd + 1) * dk]
        kh = k_ref[:, hd * dk:(hd + 1) * dk]
        vh = v_ref[:, hd * dv:(hd + 1) * dv]
        state = s_ref[0, hd]
        sc = lax.dot_general(qh, kh, (((1,), (1,)), ((), ())), preferred_element_type=F32) * dmask_ref[hd]
        o = jnp.dot(sc.astype(BF16), vh, preferred_element_type=F32)
        o = o + jnp.dot(qh, state.astype(BF16), preferred_element_type=F32) * xi_ref[hd]
        kz = (kh.astype(F32) * zeta_ref[hd]).astype(BF16)
        s_ref[0, hd] = gc_ref[hd] * state + lax.dot_general(
            kz, vh, (((0,), (0,)), ((), ())), preferred_element_type=F32)
        on = _layer_norm(o, gn_ref[hd])
        o_ref[:, hd * dv:(hd + 1) * dv] = (g_ref[:, hd * dv:(hd + 1) * dv].astype(F32) * on).astype(o_ref.dtype)


def _retention(q, k, v, g, s0_all, s_prev, dmask, xi, zeta, gc, gn, *, layer, n_layers, n_seq, n_chunk, chunk,
               out_rows, name):
    dq, dvv = q.shape[1], v.shape[1]
    blk = lambda b, c: (b * n_chunk + c, 0)
    st = lambda b, c: (layer, b, 0, 0, 0)
    state_shape = (RET_HEADS, dq // RET_HEADS, dvv // RET_HEADS)
    state_spec = pl.BlockSpec((None, 1) + state_shape, st)
    in_specs = [pl.BlockSpec((chunk, dq), blk), pl.BlockSpec((chunk, dq), blk),
                pl.BlockSpec((chunk, dvv), blk), pl.BlockSpec((chunk, dvv), blk)]
    args = [q, k, v, g]
    if s0_all is not None:
        in_specs.append(state_spec)
        args.append(s0_all)
    for t in (dmask, xi, zeta, gc, gn):
        in_specs.append(_resident(t.shape))
        args.append(t)
    aliases = {}
    if s_prev is not None:
        in_specs.append(pl.BlockSpec(memory_space=pl.ANY))
        args.append(s_prev)
        aliases = {len(args) - 1: 1}
    return pl.pallas_call(
        functools.partial(_ret_kernel, zero_init=s0_all is None),
        grid=(n_seq, n_chunk),
        in_specs=in_specs,
        out_specs=[pl.BlockSpec((chunk, dvv), blk), state_spec],
        out_shape=[jax.ShapeDtypeStruct((out_rows, dvv), BF16),
                   jax.ShapeDtypeStruct((n_layers, n_seq) + state_shape, F32)],
        input_output_aliases=aliases,
        compiler_params=_cparams(2),
        name=name,
    )(*args)


def _decay_tables(c_true, c_pad, dk, dv):
    lg = jnp.log1p(-jnp.exp2(-5.0 - jnp.arange(RET_HEADS, dtype=F32)))
    idx = jnp.arange(c_true, dtype=F32)
    diff = idx[:, None] - idx[None, :]
    dmask = jnp.where(diff >= 0, jnp.exp(lg[:, None, None] * jnp.maximum(diff, 0.0)), 0.0)
    xi = jnp.exp(lg[:, None] * (idx + 1.0))
    zeta = jnp.exp(lg[:, None] * (c_true - 1.0 - idx))
    g_c = jnp.exp(lg * c_true)
    pad = c_pad - c_true
    dmask = jnp.pad(dmask, ((0, 0), (0, pad), (0, pad)))
    xi = jnp.pad(xi, ((0, 0), (0, pad)))
    zeta = jnp.pad(zeta, ((0, 0), (0, pad)))
    return (dmask,
            jnp.broadcast_to(xi[:, :, None], (RET_HEADS, c_pad, dv)),
            jnp.broadcast_to(zeta[:, :, None], (RET_HEADS, c_pad, dk)),
            jnp.broadcast_to(g_c[:, None, None], (RET_HEADS, 1, dv)))


def kernel(x_prompt, x_sample, state_pool, state_ret, state_conv, w_mix_in, w_pool_grp, pool_scale, w_spatial,
           b_spatial, sgu_norm_g, sgu_norm_b, w_mix_out, w_q, w_k, w_v, w_g, ret_norm_g, w_ret_out, norm_mix_pre,
           norm_mix_post, norm_ffn_pre, norm_ffn_post, w_ffn_gate, w_ffn_up, w_dconv, b_dconv, w_ffn_down):
    bp, seq, d = x_prompt.shape
    bs, dec_seq, _ = x_sample.shape
    depth = norm_mix_pre.shape[0]
    n_ret = w_q.shape[0]
    a_width = w_pool_grp.shape[1] * w_pool_grp.shape[2]
    b_width = sgu_norm_g.shape[1]
    dk = w_q.shape[2] // RET_HEADS
    dv = w_v.shape[2] // RET_HEADS
    d_ff = w_ffn_gate.shape[2]
    m_p, m_s = bp * seq, bs * dec_seq
    m = m_p + m_s
    assert m_s == TM and seq % TM_FFN == 0 and seq % RET_CHUNK == 0 and seq >= POOL_BUF and a_width == b_width
    assert CONV_W - 1 <= dec_seq < POOL_BUF and dec_seq <= SGU_CHUNK and dec_seq <= BF16_SUBLANES
    n_prompt_tiles = m_p // TM
    tiles_per_seq = seq // TM
    geom = dict(n_prompt_tiles=n_prompt_tiles, tiles_per_seq=tiles_per_seq, dec_batch=bs)

    def to_rows(t):
        return t.transpose(1, 0, 2).reshape(t.shape[1] * bs, t.shape[-1])

    def from_rows(r, steps):
        return r.reshape(steps, bs, r.shape[-1]).transpose(1, 0, 2)

    x = jnp.concatenate([x_prompt.reshape(m_p, d), to_rows(x_sample)], axis=0)

    half = dk // 2
    inv = ROPE_BASE ** (-jnp.arange(half, dtype=F32) / half)
    pos_p = jnp.arange(seq).astype(F32)
    pos_s = (PAST_LEN + jnp.arange(dec_seq)).astype(F32)
    ang = jnp.concatenate([jnp.tile(pos_p[:, None] * inv[None, :], (bp, 1)),
                           jnp.repeat(pos_s[:, None] * inv[None, :], bs, axis=0)], axis=0)
    rot = (jnp.cos(ang), jnp.sin(ang))
    rot_specs = (pl.BlockSpec((TM_PROJ, half), lambda j, i: (i, 0)),) * 2

    pos = jnp.arange(seq)
    invc = jnp.concatenate(
        [jnp.broadcast_to((1.0 / jnp.minimum(pos + 1, w).astype(F32))[:, None], (seq, LANES)) for w in POOL_WINDOWS],
        axis=1)

    tril_p = jnp.tril(jnp.ones((SGU_CHUNK, SGU_CHUNK), dtype=bool))
    tril_s = jnp.tril(jnp.ones((dec_seq, dec_seq), dtype=bool))
    eye_b = jnp.eye(bs, dtype=F32)
    hdim = b_width // SGU_HEADS

    dec_pad = BF16_SUBLANES
    tabs_p = _decay_tables(RET_CHUNK, RET_CHUNK, dk, dv)
    tabs_s = _decay_tables(dec_seq, dec_pad, dk, dv)

    def pad_steps(r):
        t = from_rows(r, dec_seq)
        return jnp.pad(t, ((0, 0), (0, dec_pad - dec_seq), (0, 0))).reshape(bs * dec_pad, r.shape[-1])

    conv0_p = jnp.zeros((SUBLANES, d_ff), F32)
    pool_p, pool_s, vn_s, conv_p, conv_s = [], [], [], [], []
    ret_p = ret_s = None
    h = _norm(x, norm_mix_pre[0])
    for l in range(depth):
        if l % 2 == 0:
            e = l // 2
            pw = dict(layer=e, tn=a_width, n_out=a_width)
            a = _proj(h, w_mix_in, col0=0, out_dtype=F32, epilogue="none", name="proj_a", **pw)
            u = _proj(h, w_mix_in, col0=1, out_dtype=BF16, epilogue="gelu", name="proj_u", **pw)
            vn = _proj(h, w_mix_in, col0=2, out_dtype=F32, epilogue="gelu_ln",
                       extras=(sgu_norm_g[e].reshape(1, b_width), sgu_norm_b[e].reshape(1, b_width)),
                       extra_specs=(pl.BlockSpec((1, b_width), lambda j, i: (0, 0)),) * 2, name="proj_v", **pw)
            ws_p = jnp.where(tril_p, w_spatial[e][:, :SGU_CHUNK, :SGU_CHUNK], 0.0)
            ws_s = jnp.where(tril_s, w_spatial[e][:, :dec_seq, :dec_seq], 0.0)
            wmix_s = jnp.einsum("hij,bc->hibjc", ws_s, eye_b).reshape(SGU_HEADS, m_s, m_s)
            bias_p = jnp.broadcast_to(b_spatial[e][:, :SGU_CHUNK, None], (SGU_HEADS, SGU_CHUNK, hdim))
            bias_s = jnp.broadcast_to(b_spatial[e][:, :dec_seq, None, None],
                                      (SGU_HEADS, dec_seq, bs, hdim)).reshape(SGU_HEADS, m_s, hdim)
            halo_s = state_pool[e].transpose(1, 0, 2).reshape(POOL_BUF * bs, a_width)
            mixed = _mix(a, u, vn, invc, halo_s, w_pool_grp[e].astype(BF16), pool_scale[e].reshape(1, a_width),
                         ws_p.astype(BF16), bias_p, wmix_s.astype(BF16), bias_s, **geom)
            w_o = w_mix_out[e].astype(BF16)
            pool_p.append(a[:m_p].reshape(bp, seq, a_width)[:, seq - POOL_BUF:])
            pool_s.append(jnp.concatenate([state_pool[e][:, dec_seq:], from_rows(a[m_p:], dec_seq)], axis=1))
            vn_s.append(from_rows(vn[m_p:], dec_seq))
        else:
            r = l // 2
            pw = dict(layer=r, col0=0, tn=1024, out_dtype=BF16)
            q = _proj(h, w_q, n_out=w_q.shape[2], epilogue="rotary", extras=rot, extra_specs=rot_specs,
                      name="proj_q", **pw)
            k = _proj(h, w_k, n_out=w_k.shape[2], epilogue="rotary", extras=rot, extra_specs=rot_specs,
                      scale=dk ** -0.5, name="proj_k", **pw)
            v = _proj(h, w_v, n_out=w_v.shape[2], epilogue="none", name="proj_v_ret", **pw)
            g = _proj(h, w_g, n_out=w_g.shape[2], epilogue="silu", name="proj_g", **pw)
            gn = ret_norm_g[r].reshape(RET_HEADS, 1, dv)
            gated, ret_p = _retention(q, k, v, g, None, ret_p, *tabs_p, gn, layer=r, n_layers=n_ret, n_seq=bp,
                                      n_chunk=seq // RET_CHUNK, chunk=RET_CHUNK, out_rows=m, name="ret_prompt")
            gated_s, ret_s = _retention(pad_steps(q[m_p:]), pad_steps(k[m_p:]), pad_steps(v[m_p:]),
                                        pad_steps(g[m_p:]), state_ret, ret_s, *tabs_s, gn, layer=r, n_layers=n_ret,
                                        n_seq=bs, n_chunk=1, chunk=dec_pad, out_rows=bs * dec_pad, name="ret_sample")
            gated_s = to_rows(gated_s.reshape(bs, dec_pad, -1)[:, :dec_seq])
            mixed = lax.dynamic_update_slice(gated, gated_s, (m_p, 0))
            w_o = w_ret_out[r].astype(BF16)
        x, h = _out_proj(mixed, w_o, x, norm_mix_post[l], norm_ffn_pre[l], tm=TM_OUT, name="mix_out")
        ffn = dict(layer=l, tn=TN_FFN)
        wc, bc = w_dconv, b_dconv.reshape(depth, 1, d_ff)
        act, tail_p = _ffn_in(h, w_ffn_gate, w_ffn_up, wc, bc, conv0_p, None, tm=TM_FFN, row0=0,
                              n_tiles=m_p // TM_FFN, tiles_per_seq=seq // TM_FFN, shift=1, **ffn)
        act, tail_s = _ffn_in(h, w_ffn_gate, w_ffn_up, wc, bc, to_rows(state_conv[l]), act, tm=m_s,
                              row0=m_p // m_s, n_tiles=1, tiles_per_seq=1, shift=bs, **ffn)
        conv_p.append(tail_p.reshape(bp, seq // TM_FFN, SUBLANES, d_ff)[:, -1, SUBLANES - (CONV_W - 1):])
        conv_s.append(from_rows(tail_s, CONV_W - 1))
        g_next = norm_mix_pre[l + 1] if l + 1 < depth else None
        x, h = _out_proj(act, w_ffn_down[l].astype(BF16), x, norm_ffn_post[l], g_next, tm=TM_DOWN, name="ffn_out")

    y_prompt = x[:m_p].reshape(bp, seq, d)
    y_sample = from_rows(x[m_p:], dec_seq)
    return (y_prompt, y_sample, jnp.stack(pool_p), jnp.stack(pool_s), jnp.stack(vn_s),
            ret_p, ret_s, jnp.stack(conv_p), jnp.stack(conv_s))
```

```python
import functools

import jax
import jax.numpy as jnp
from jax import lax
from jax.experimental import pallas as pl
from jax.experimental.pallas import tpu as pltpu

F32 = jnp.float32
BF16 = jnp.bfloat16

PAST_LEN = 16384
POOL_WINDOWS = (2, 4, 8, 16)
POOL_BUF = max(POOL_WINDOWS) - 1
SGU_HEADS = 4
SGU_CHUNK = 128
RET_HEADS = 8
RET_CHUNK = 128
ROPE_BASE = 10000.0
CONV_W = 3
EPS = 1e-6

V7X_VMEM_BYTES = 64 * 1024 * 1024
VMEM_LIMIT_BYTES = V7X_VMEM_BYTES - 8 * 1024 * 1024
SUBLANES = 8
LANES = 128
BF16_SUBLANES = 16

TM = 512
TM_PROJ = 1088
TM_FFN = 1024
TN_FFN = 512
TM_OUT = 256
TM_DOWN = 256
KC_OUT = 256


def _cparams(n_axes):
    return pltpu.CompilerParams(
        dimension_semantics=("arbitrary",) * n_axes, vmem_limit_bytes=VMEM_LIMIT_BYTES)


def _resident(shape):
    zeros = (0,) * len(shape)
    return pl.BlockSpec(shape, lambda *_: zeros, pipeline_mode=pl.Buffered(1))


def _rms(x, g):
    return x * lax.rsqrt(jnp.mean(x * x, axis=-1, keepdims=True) + EPS) * g


def _layer_norm(x, g):
    mu = jnp.mean(x, axis=-1, keepdims=True)
    xc = x - mu
    return xc * lax.rsqrt(jnp.mean(xc * xc, axis=-1, keepdims=True) + EPS) * g


def _first_spec(tm, d, n_first):
    return pl.BlockSpec((tm, d), lambda i: (jnp.minimum(i, n_first - 1), 0))


def _second_spec(tm, d, n_first):
    return pl.BlockSpec((tm, d), lambda i: (jnp.maximum(i - n_first, 0), 0))


def _norm_kernel(xp_ref, xs_ref, g_ref, h_ref, *, n_first):
    i = pl.program_id(0)

    @pl.when(i < n_first)
    def _():
        h_ref[...] = _rms(xp_ref[...], g_ref[...]).astype(BF16)

    @pl.when(i >= n_first)
    def _():
        h_ref[...] = _rms(xs_ref[...], g_ref[...]).astype(BF16)


def _norm(xp, xs, g):
    d = xp.shape[1]
    m = xp.shape[0] + xs.shape[0]
    n_first = xp.shape[0] // TM
    return pl.pallas_call(
        functools.partial(_norm_kernel, n_first=n_first),
        grid=(m // TM,),
        in_specs=[_first_spec(TM, d, n_first), _second_spec(TM, d, n_first), _resident((1, d))],
        out_specs=pl.BlockSpec((TM, d), lambda i: (i, 0)),
        out_shape=jax.ShapeDtypeStruct((m, d), BF16),
        compiler_params=_cparams(1),
        name="norm",
    )(xp, xs, g.reshape(1, d))


def _proj_kernel(h_ref, w_ref, *rest, epilogue, scale):
    *extras, o_ref, wb_ref = rest

    @pl.when(pl.program_id(1) == 0)
    def _():
        wb_ref[...] = w_ref[...].astype(BF16)

    z = jnp.dot(h_ref[...], wb_ref[...], preferred_element_type=F32)
    if epilogue == "none":
        o_ref[...] = z.astype(o_ref.dtype)
    elif epilogue == "gelu":
        o_ref[...] = jax.nn.gelu(z, approximate=True).astype(o_ref.dtype)
    elif epilogue == "silu":
        o_ref[...] = jax.nn.silu(z).astype(o_ref.dtype)
    elif epilogue == "gelu_ln":
        g_ref, b_ref = extras
        v = jax.nn.gelu(z, approximate=True)
        o_ref[...] = (_layer_norm(v, g_ref[...]) + b_ref[...]).astype(o_ref.dtype)
    elif epilogue == "rotary":
        cos_ref, sin_ref = extras
        c = cos_ref[...]
        s = sin_ref[...]
        half = c.shape[-1]
        for hd in range(z.shape[-1] // (2 * half)):
            lo = hd * 2 * half
            x1 = z[:, lo:lo + half]
            x2 = z[:, lo + half:lo + 2 * half]
            o_ref[:, lo:lo + half] = ((x1 * c - x2 * s) * scale).astype(o_ref.dtype)
            o_ref[:, lo + half:lo + 2 * half] = ((x1 * s + x2 * c) * scale).astype(o_ref.dtype)
    else:
        raise ValueError(epilogue)


def _proj(h, w, *, layer, col0, n_out, tn, out_dtype, epilogue, extras=(), extra_specs=(), scale=1.0, name):
    m, k = h.shape
    assert m % TM_PROJ == 0 and n_out % tn == 0
    return pl.pallas_call(
        functools.partial(_proj_kernel, epilogue=epilogue, scale=scale),
        grid=(n_out // tn, m // TM_PROJ),
        in_specs=[pl.BlockSpec((TM_PROJ, k), lambda j, i: (i, 0)),
                  pl.BlockSpec((None, k, tn), lambda j, i: (layer, 0, col0 + j)),
                  *extra_specs],
        out_specs=pl.BlockSpec((TM_PROJ, tn), lambda j, i: (i, j)),
        out_shape=jax.ShapeDtypeStruct((m, n_out), out_dtype),
        scratch_shapes=[pltpu.VMEM((k, tn), BF16)],
        compiler_params=_cparams(2),
        name=name,
    )(h, w, *extras)


def _out_kernel(*refs, layer, n_first, split_lhs, split_x, last):
    refs = list(refs)
    wb_ref, stage_ref, sem_ref = refs[-3:]
    lhs_refs = [refs.pop(0) for _ in range(2 if split_lhs else 1)]
    w_hbm_ref = refs.pop(0)
    x_refs = [refs.pop(0) for _ in range(2 if split_x else 1)]
    gpost_ref, gnext_ref, *outs = refs[:-3]
    i = pl.program_id(0)
    k = lhs_refs[0].shape[1]
    kc = stage_ref.shape[1]
    n_chunks = k // kc

    def pick(pair):
        return jnp.where(i < n_first, pair[0][...], pair[1][...]) if len(pair) == 2 else pair[0][...]

    def w_copy(c):
        slot = c % 2
        return pltpu.make_async_copy(w_hbm_ref.at[layer, pl.ds(c * kc, kc), :], stage_ref.at[slot], sem_ref.at[slot])

    @pl.when(i == 0)
    def _():
        w_copy(0).start()
        for c in range(n_chunks):
            if c + 1 < n_chunks:
                w_copy(c + 1).start()
            w_copy(c).wait()
            wb_ref[c * kc:(c + 1) * kc, :] = stage_ref[c % 2].astype(BF16)

    y = jnp.dot(pick(lhs_refs), wb_ref[...], preferred_element_type=F32)
    x_new = pick(x_refs) + _rms(y, gpost_ref[...])
    if last:
        yp_ref, ys_ref = outs

        @pl.when(i < n_first)
        def _():
            yp_ref[...] = x_new

        @pl.when(i >= n_first)
        def _():
            ys_ref[...] = x_new
    else:
        xo_ref, h_ref = outs
        xo_ref[...] = x_new
        h_ref[...] = _rms(x_new, gnext_ref[...]).astype(BF16)


def _out_proj(lhs, w_all, x, g_post, g_next, *, layer, tm, m_first, name):
    k, d = w_all.shape[1:]
    n_first = m_first // tm
    split_lhs, split_x = isinstance(lhs, tuple), isinstance(x, tuple)
    m = sum(t.shape[0] for t in lhs) if split_lhs else lhs.shape[0]
    last = g_next is None

    def pair(width):
        return [_first_spec(tm, width, n_first), _second_spec(tm, width, n_first)]

    def operand(t, width):
        return (list(t), pair(width)) if isinstance(t, tuple) else ([t], [pl.BlockSpec((tm, width), lambda i: (i, 0))])

    lhs_args, lhs_specs = operand(lhs, k)
    x_args, x_specs = operand(x, d)
    if last:
        g_next = g_post
        out_specs = pair(d)
        out_shape = [jax.ShapeDtypeStruct((m_first, d), F32), jax.ShapeDtypeStruct((m - m_first, d), F32)]
    else:
        out_specs = [pl.BlockSpec((tm, d), lambda i: (i, 0))] * 2
        out_shape = [jax.ShapeDtypeStruct((m, d), F32), jax.ShapeDtypeStruct((m, d), BF16)]
    res = pl.pallas_call(
        functools.partial(_out_kernel, layer=layer, n_first=n_first, split_lhs=split_lhs, split_x=split_x, last=last),
        grid=(m // tm,),
        in_specs=[*lhs_specs, pl.BlockSpec(memory_space=pl.ANY), *x_specs, _resident((1, d)), _resident((1, d))],
        out_specs=out_specs,
        out_shape=out_shape,
        scratch_shapes=[pltpu.VMEM((k, d), BF16), pltpu.VMEM((2, KC_OUT, d), F32), pltpu.SemaphoreType.DMA((2,))],
        compiler_params=_cparams(1),
        name=name,
    )(*lhs_args, w_all, *x_args, g_post.reshape(1, d), g_next.reshape(1, d))
    return ((res[0], res[1]), None) if last else (res[0], res[1])


def _mix_group(a_ref, u_ref, vn_ref, wgrp_ref, pscale_ref, wmix_ref, bias_ref, o_ref, ext_ref,
               inv_cnt, halo_rows, shift, chunk):
    tm, a_width = a_ref.shape
    gdim = a_width // len(POOL_WINDOWS)
    ext_ref[halo_rows:halo_rows + tm, :] = a_ref[...]
    for gi, w in enumerate(POOL_WINDOWS):
        c0, c1 = gi * gdim, (gi + 1) * gdim
        s = ext_ref[halo_rows:halo_rows + tm, c0:c1]
        for j in range(1, w):
            s = s + ext_ref[halo_rows - j * shift:halo_rows - j * shift + tm, c0:c1]
        d = (s * inv_cnt(gi) - a_ref[:, c0:c1]).astype(BF16)
        z = jnp.dot(d, wgrp_ref[gi], preferred_element_type=F32)
        o_ref[:, c0:c1] = (z * pscale_ref[:, c0:c1]).astype(o_ref.dtype)
    hdim = vn_ref.shape[1] // SGU_HEADS
    for c in range(tm // chunk):
        r0, r1 = c * chunk, (c + 1) * chunk
        for hd in range(SGU_HEADS):
            c0, c1 = hd * hdim, (hd + 1) * hdim
            mixed = jnp.dot(wmix_ref[hd], vn_ref[r0:r1, c0:c1].astype(BF16),
                            preferred_element_type=F32) + bias_ref[hd]
            o_ref[r0:r1, a_width + c0:a_width + c1] = (
                u_ref[r0:r1, c0:c1].astype(F32) * mixed).astype(o_ref.dtype)


def _mix_kernel(a_ref, u_ref, vn_ref, invc_ref, halo_s_ref, wgrp_ref, pscale_ref,
                wmix_p_ref, bias_p_ref, wmix_s_ref, bias_s_ref, o_ref, ext_ref,
                *, n_prompt_tiles, tiles_per_seq, dec_batch):
    i = pl.program_id(0)
    tm = a_ref.shape[0]
    gl = LANES
    halo_p = 2 * SUBLANES

    @pl.when(i < n_prompt_tiles)
    def _prompt():
        @pl.when(i % tiles_per_seq == 0)
        def _():
            ext_ref[0:halo_p, :] = jnp.zeros((halo_p, ext_ref.shape[1]), F32)

        def inv_cnt(gi):
            blk = invc_ref[:, gi * gl:(gi + 1) * gl]
            return jnp.concatenate([blk, blk], axis=1)

        _mix_group(a_ref, u_ref, vn_ref, wgrp_ref, pscale_ref, wmix_p_ref, bias_p_ref, o_ref, ext_ref,
                   inv_cnt, halo_p, 1, SGU_CHUNK)
        ext_ref[0:halo_p, :] = ext_ref[tm:tm + halo_p, :]

    @pl.when(i >= n_prompt_tiles)
    def _sample():
        halo_s = POOL_BUF * dec_batch
        ext_ref[0:halo_s, :] = halo_s_ref[...]
        _mix_group(a_ref, u_ref, vn_ref, wgrp_ref, pscale_ref, wmix_s_ref, bias_s_ref, o_ref, ext_ref,
                   lambda gi: 1.0 / POOL_WINDOWS[gi], halo_s, dec_batch, tm)


def _mix(a, u, vn, invc, halo_s, wgrp, pscale, wmix_p, bias_p, wmix_s, bias_s, *, n_prompt_tiles, tiles_per_seq,
         dec_batch):
    m, a_width = a.shape
    b_width = u.shape[1]
    row = lambda i: (i, 0)
    ext_rows = max(2 * SUBLANES, POOL_BUF * dec_batch) + TM
    return pl.pallas_call(
        functools.partial(_mix_kernel, n_prompt_tiles=n_prompt_tiles, tiles_per_seq=tiles_per_seq,
                          dec_batch=dec_batch),
        grid=(m // TM,),
        in_specs=[pl.BlockSpec((TM, a_width), row), pl.BlockSpec((TM, b_width), row),
                  pl.BlockSpec((TM, b_width), row),
                  pl.BlockSpec((TM, invc.shape[1]), lambda i: (jnp.minimum(i, n_prompt_tiles - 1) % tiles_per_seq, 0)),
                  _resident(halo_s.shape), _resident(wgrp.shape), _resident(pscale.shape),
                  _resident(wmix_p.shape), _resident(bias_p.shape), _resident(wmix_s.shape),
                  _resident(bias_s.shape)],
        out_specs=pl.BlockSpec((TM, a_width + b_width), row),
        out_shape=jax.ShapeDtypeStruct((m, a_width + b_width), BF16),
        scratch_shapes=[pltpu.VMEM((ext_rows, a_width), F32)],
        compiler_params=_cparams(1),
        name="mix",
    )(a, u, vn, invc, halo_s, wgrp, pscale, wmix_p, bias_p, wmix_s, bias_s)


def _ffn_in_kernel(h_ref, wg_ref, wu_ref, wc_ref, bc_ref, cstate_ref, *rest, shift, tiles_per_seq):
    act_ref, tail_ref, wgb_ref, wub_ref, ext_ref = rest
    i = pl.program_id(1)
    tm = h_ref.shape[0]
    halo = cstate_ref.shape[0]

    @pl.when(i == 0)
    def _():
        wgb_ref[...] = wg_ref[...].astype(BF16)
        wub_ref[...] = wu_ref[...].astype(BF16)
        ext_ref[0:halo, :] = cstate_ref[...]

    h = h_ref[...]
    gate = jnp.dot(h, wgb_ref[...], preferred_element_type=F32)
    up = jnp.dot(h, wub_ref[...], preferred_element_type=F32)
    if tiles_per_seq > 1:
        seq_start = (i % tiles_per_seq) == 0
        ext_ref[0:halo, :] = jnp.where(seq_start, cstate_ref[...], ext_ref[0:halo, :])
    ext_ref[halo:halo + tm, :] = gate
    conv = bc_ref[...] + ext_ref[halo - 2 * shift:halo - 2 * shift + tm, :] * wc_ref[0:1, :]
    conv = conv + ext_ref[halo - shift:halo - shift + tm, :] * wc_ref[1:2, :]
    conv = conv + gate * wc_ref[2:3, :]
    act_ref[...] = (jax.nn.gelu(conv, approximate=True) * up).astype(act_ref.dtype)
    tail = gate[tm - halo:tm, :]
    tail_ref[...] = tail
    if tiles_per_seq > 1:
        ext_ref[0:halo, :] = tail


def _ffn_in(h, wg, wu, wc, bc, cstate, *, layer, tm, tn, row0, n_tiles, tiles_per_seq, shift, name):
    m, k = h.shape
    n = wg.shape[2]
    halo = cstate.shape[0]
    wspec = pl.BlockSpec((None, k, tn), lambda j, i: (layer, 0, j))
    in_specs = [pl.BlockSpec((tm, k), lambda j, i: (row0 + i, 0)), wspec, wspec,
                pl.BlockSpec((None, CONV_W, tn), lambda j, i: (layer, 0, j)),
                pl.BlockSpec((None, 1, tn), lambda j, i: (layer, 0, j)),
                pl.BlockSpec((halo, tn), lambda j, i: (0, j))]
    return pl.pallas_call(
        functools.partial(_ffn_in_kernel, shift=shift, tiles_per_seq=tiles_per_seq),
        grid=(n // tn, n_tiles),
        in_specs=in_specs,
        out_specs=[pl.BlockSpec((tm, tn), lambda j, i: (i, j)),
                   pl.BlockSpec((halo, tn), lambda j, i: (i, j))],
        out_shape=[jax.ShapeDtypeStruct((n_tiles * tm, n), BF16),
                   jax.ShapeDtypeStruct((n_tiles * halo, n), F32)],
        scratch_shapes=[pltpu.VMEM((k, tn), BF16), pltpu.VMEM((k, tn), BF16), pltpu.VMEM((halo + tm, tn), F32)],
        compiler_params=_cparams(2),
        name=name,
    )(h, wg, wu, wc, bc, cstate)


def _ret_kernel(q_ref, k_ref, v_ref, g_ref, *rest, zero_init):
    s_ref = rest[-1]
    o_ref = rest[-2]
    if zero_init:
        dmask_ref, xi_ref, zeta_ref, gc_ref, gn_ref = rest[:5]
    else:
        s0_ref, dmask_ref, xi_ref, zeta_ref, gc_ref, gn_ref = rest[:6]

    @pl.when(pl.program_id(1) == 0)
    def _():
        s_ref[...] = jnp.zeros(s_ref.shape, F32) if zero_init else s0_ref[...]

    dk = q_ref.shape[1] // RET_HEADS
    dv = v_ref.shape[1] // RET_HEADS
    for hd in range(RET_HEADS):
        qh = q_ref[:, hd * dk:(hd + 1) * dk]
        kh = k_ref[:, hd * dk:(hd + 1) * dk]
        vh = v_ref[:, hd * dv:(hd + 1) * dv]
        state = s_ref[0, hd]
        sc = lax.dot_general(qh, kh, (((1,), (1,)), ((), ())), preferred_element_type=F32) * dmask_ref[hd]
        o = jnp.dot(sc.astype(BF16), vh, preferred_element_type=F32)
        o = o + jnp.dot(qh, state.astype(BF16), preferred_element_type=F32) * xi_ref[hd]
        kz = (kh.astype(F32) * zeta_ref[hd]).astype(BF16)
        s_ref[0, hd] = gc_ref[hd] * state + lax.dot_general(
            kz, vh, (((0,), (0,)), ((), ())), preferred_element_type=F32)
        on = _layer_norm(o, gn_ref[hd])
        o_ref[:, hd * dv:(hd + 1) * dv] = (g_ref[:, hd * dv:(hd + 1) * dv].astype(F32) * on).astype(o_ref.dtype)


def _retention(q, k, v, g, s0_all, s_prev, dmask, xi, zeta, gc, gn, *, layer, n_layers, n_seq, n_chunk, chunk,
               out_rows, name):
    dq, dvv = q.shape[1], v.shape[1]
    blk = lambda b, c: (b * n_chunk + c, 0)
    st = lambda b, c: (layer, b, 0, 0, 0)
    state_shape = (RET_HEADS, dq // RET_HEADS, dvv // RET_HEADS)
    state_spec = pl.BlockSpec((None, 1) + state_shape, st)
    in_specs = [pl.BlockSpec((chunk, dq), blk), pl.BlockSpec((chunk, dq), blk),
                pl.BlockSpec((chunk, dvv), blk), pl.BlockSpec((chunk, dvv), blk)]
    args = [q, k, v, g]
    if s0_all is not None:
        in_specs.append(state_spec)
        args.append(s0_all)
    for t in (dmask, xi, zeta, gc, gn):
        in_specs.append(_resident(t.shape))
        args.append(t)
    aliases = {}
    if s_prev is not None:
        in_specs.append(pl.BlockSpec(memory_space=pl.ANY))
        args.append(s_prev)
        aliases = {len(args) - 1: 1}
    return pl.pallas_call(
        functools.partial(_ret_kernel, zero_init=s0_all is None),
        grid=(n_seq, n_chunk),
        in_specs=in_specs,
        out_specs=[pl.BlockSpec((chunk, dvv), blk), state_spec],
        out_shape=[jax.ShapeDtypeStruct((out_rows, dvv), BF16),
                   jax.ShapeDtypeStruct((n_layers, n_seq) + state_shape, F32)],
        input_output_aliases=aliases,
        compiler_params=_cparams(2),
        name=name,
    )(*args)


def _decay_tables(c_true, c_pad, dk, dv):
    lg = jnp.log1p(-jnp.exp2(-5.0 - jnp.arange(RET_HEADS, dtype=F32)))
    idx = jnp.arange(c_true, dtype=F32)
    diff = idx[:, None] - idx[None, :]
    dmask = jnp.where(diff >= 0, jnp.exp(lg[:, None, None] * jnp.maximum(diff, 0.0)), 0.0)
    xi = jnp.exp(lg[:, None] * (idx + 1.0))
    zeta = jnp.exp(lg[:, None] * (c_true - 1.0 - idx))
    g_c = jnp.exp(lg * c_true)
    pad = c_pad - c_true
    dmask = jnp.pad(dmask, ((0, 0), (0, pad), (0, pad)))
    xi = jnp.pad(xi, ((0, 0), (0, pad)))
    zeta = jnp.pad(zeta, ((0, 0), (0, pad)))
    return (dmask,
            jnp.broadcast_to(xi[:, :, None], (RET_HEADS, c_pad, dv)),
            jnp.broadcast_to(zeta[:, :, None], (RET_HEADS, c_pad, dk)),
            jnp.broadcast_to(g_c[:, None, None], (RET_HEADS, 1, dv)))


def kernel(x_prompt, x_sample, state_pool, state_ret, state_conv, w_mix_in, w_pool_grp, pool_scale, w_spatial,
           b_spatial, sgu_norm_g, sgu_norm_b, w_mix_out, w_q, w_k, w_v, w_g, ret_norm_g, w_ret_out, norm_mix_pre,
           norm_mix_post, norm_ffn_pre, norm_ffn_post, w_ffn_gate, w_ffn_up, w_dconv, b_dconv, w_ffn_down):
    bp, seq, d = x_prompt.shape
    bs, dec_seq, _ = x_sample.shape
    depth = norm_mix_pre.shape[0]
    n_ret = w_q.shape[0]
    a_width = w_pool_grp.shape[1] * w_pool_grp.shape[2]
    b_width = sgu_norm_g.shape[1]
    dk = w_q.shape[2] // RET_HEADS
    dv = w_v.shape[2] // RET_HEADS
    d_ff = w_ffn_gate.shape[2]
    m_p, m_s = bp * seq, bs * dec_seq
    m = m_p + m_s
    assert m_s == TM and seq % TM_FFN == 0 and seq % RET_CHUNK == 0 and seq >= POOL_BUF and a_width == b_width
    assert CONV_W - 1 <= dec_seq < POOL_BUF and dec_seq <= SGU_CHUNK and dec_seq <= BF16_SUBLANES
    n_prompt_tiles = m_p // TM
    tiles_per_seq = seq // TM
    geom = dict(n_prompt_tiles=n_prompt_tiles, tiles_per_seq=tiles_per_seq, dec_batch=bs)

    def to_rows(t):
        return t.transpose(1, 0, 2).reshape(t.shape[1] * bs, t.shape[-1])

    def from_rows(r, steps):
        return r.reshape(steps, bs, r.shape[-1]).transpose(1, 0, 2)

    x = (x_prompt.reshape(m_p, d), to_rows(x_sample))

    half = dk // 2
    inv = ROPE_BASE ** (-jnp.arange(half, dtype=F32) / half)
    pos_p = jnp.arange(seq).astype(F32)
    pos_s = (PAST_LEN + jnp.arange(dec_seq)).astype(F32)
    ang = jnp.concatenate([jnp.tile(pos_p[:, None] * inv[None, :], (bp, 1)),
                           jnp.repeat(pos_s[:, None] * inv[None, :], bs, axis=0)], axis=0)
    rot = (jnp.cos(ang), jnp.sin(ang))
    rot_specs = (pl.BlockSpec((TM_PROJ, half), lambda j, i: (i, 0)),) * 2

    pos = jnp.arange(seq)
    invc = jnp.concatenate(
        [jnp.broadcast_to((1.0 / jnp.minimum(pos + 1, w).astype(F32))[:, None], (seq, LANES)) for w in POOL_WINDOWS],
        axis=1)

    tril_p = jnp.tril(jnp.ones((SGU_CHUNK, SGU_CHUNK), dtype=bool))
    tril_s = jnp.tril(jnp.ones((dec_seq, dec_seq), dtype=bool))
    eye_b = jnp.eye(bs, dtype=F32)
    hdim = b_width // SGU_HEADS

    dec_pad = BF16_SUBLANES
    tabs_p = _decay_tables(RET_CHUNK, RET_CHUNK, dk, dv)
    tabs_s = _decay_tables(dec_seq, dec_pad, dk, dv)

    def pad_steps(r):
        t = from_rows(r, dec_seq)
        return jnp.pad(t, ((0, 0), (0, dec_pad - dec_seq), (0, 0))).reshape(bs * dec_pad, r.shape[-1])

    conv0_p = jnp.zeros((SUBLANES, d_ff), F32)
    pool_p, pool_s, vn_s, conv_p, conv_s = [], [], [], [], []
    ret_p = ret_s = None
    h = _norm(*x, norm_mix_pre[0])
    for l in range(depth):
        if l % 2 == 0:
            e = l // 2
            pw = dict(layer=e, tn=a_width, n_out=a_width)
            a = _proj(h, w_mix_in, col0=0, out_dtype=F32, epilogue="none", name="proj_a", **pw)
            u = _proj(h, w_mix_in, col0=1, out_dtype=BF16, epilogue="gelu", name="proj_u", **pw)
            vn = _proj(h, w_mix_in, col0=2, out_dtype=F32, epilogue="gelu_ln",
                       extras=(sgu_norm_g[e].reshape(1, b_width), sgu_norm_b[e].reshape(1, b_width)),
                       extra_specs=(pl.BlockSpec((1, b_width), lambda j, i: (0, 0)),) * 2, name="proj_v", **pw)
            ws_p = jnp.where(tril_p, w_spatial[e][:, :SGU_CHUNK, :SGU_CHUNK], 0.0)
            ws_s = jnp.where(tril_s, w_spatial[e][:, :dec_seq, :dec_seq], 0.0)
            wmix_s = jnp.einsum("hij,bc->hibjc", ws_s, eye_b).reshape(SGU_HEADS, m_s, m_s)
            bias_p = jnp.broadcast_to(b_spatial[e][:, :SGU_CHUNK, None], (SGU_HEADS, SGU_CHUNK, hdim))
            bias_s = jnp.broadcast_to(b_spatial[e][:, :dec_seq, None, None],
                                      (SGU_HEADS, dec_seq, bs, hdim)).reshape(SGU_HEADS, m_s, hdim)
            halo_s = state_pool[e].transpose(1, 0, 2).reshape(POOL_BUF * bs, a_width)
            mixed = _mix(a, u, vn, invc, halo_s, w_pool_grp[e].astype(BF16), pool_scale[e].reshape(1, a_width),
                         ws_p.astype(BF16), bias_p, wmix_s.astype(BF16), bias_s, **geom)
            w_o, lw = w_mix_out, e
            pool_p.append(jnp.stack([a[(b + 1) * seq - POOL_BUF:(b + 1) * seq] for b in range(bp)]))
            pool_s.append(jnp.concatenate([state_pool[e][:, dec_seq:], from_rows(a[m_p:], dec_seq)], axis=1))
            vn_s.append(from_rows(vn[m_p:], dec_seq))
        else:
            r = l // 2
            pw = dict(layer=r, col0=0, tn=1024, out_dtype=BF16)
            q = _proj(h, w_q, n_out=w_q.shape[2], epilogue="rotary", extras=rot, extra_specs=rot_specs,
                      name="proj_q", **pw)
            k = _proj(h, w_k, n_out=w_k.shape[2], epilogue="rotary", extras=rot, extra_specs=rot_specs,
                      scale=dk ** -0.5, name="proj_k", **pw)
            v = _proj(h, w_v, n_out=w_v.shape[2], epilogue="none", name="proj_v_ret", **pw)
            g = _proj(h, w_g, n_out=w_g.shape[2], epilogue="silu", name="proj_g", **pw)
            gn = ret_norm_g[r].reshape(RET_HEADS, 1, dv)
            gated, ret_p = _retention(q, k, v, g, None, ret_p, *tabs_p, gn, layer=r, n_layers=n_ret, n_seq=bp,
                                      n_chunk=seq // RET_CHUNK, chunk=RET_CHUNK, out_rows=m_p, name="ret_prompt")
            gated_s, ret_s = _retention(pad_steps(q[m_p:]), pad_steps(k[m_p:]), pad_steps(v[m_p:]),
                                        pad_steps(g[m_p:]), state_ret, ret_s, *tabs_s, gn, layer=r, n_layers=n_ret,
                                        n_seq=bs, n_chunk=1, chunk=dec_pad, out_rows=bs * dec_pad, name="ret_sample")
            gated_s = to_rows(gated_s.reshape(bs, dec_pad, -1)[:, :dec_seq])
            mixed = (gated, gated_s)
            w_o, lw = w_ret_out, r
        x, h = _out_proj(mixed, w_o, x, norm_mix_post[l], norm_ffn_pre[l], layer=lw, tm=TM_OUT, m_first=m_p,
                         name="mix_out")
        ffn = dict(layer=l, tn=TN_FFN)
        wc, bc = w_dconv, b_dconv.reshape(depth, 1, d_ff)
        act_p, tail_p = _ffn_in(h, w_ffn_gate, w_ffn_up, wc, bc, conv0_p, tm=TM_FFN, row0=0, n_tiles=m_p // TM_FFN,
                                tiles_per_seq=seq // TM_FFN, shift=1, name="ffn_in_prompt", **ffn)
        act_s, tail_s = _ffn_in(h, w_ffn_gate, w_ffn_up, wc, bc, to_rows(state_conv[l]), tm=m_s, row0=m_p // m_s,
                                n_tiles=1, tiles_per_seq=1, shift=bs, name="ffn_in_sample", **ffn)
        conv_p.append(tail_p.reshape(bp, seq // TM_FFN, SUBLANES, d_ff)[:, -1, SUBLANES - (CONV_W - 1):])
        conv_s.append(from_rows(tail_s, CONV_W - 1))
        g_next = norm_mix_pre[l + 1] if l + 1 < depth else None
        x, h = _out_proj((act_p, act_s), w_ffn_down, x, norm_ffn_post[l], g_next, layer=l, tm=TM_DOWN, m_first=m_p,
                         name="ffn_out")

    y_prompt = x[0].reshape(bp, seq, d)
    y_sample = from_rows(x[1], dec_seq)
    return (y_prompt, y_sample, jnp.stack(pool_p), jnp.stack(pool_s), jnp.stack(vn_s),
            ret_p, ret_s, jnp.stack(conv_p), jnp.stack(conv_s))
```

```python
import functools

import jax
import jax.numpy as jnp
import numpy as np
from jax import lax
from jax.experimental import pallas as pl
from jax.experimental.pallas import tpu as pltpu

F32 = jnp.float32
BF16 = jnp.bfloat16

PAST_LEN = 16384
POOL_WINDOWS = (2, 4, 8, 16)
POOL_BUF = max(POOL_WINDOWS) - 1
SGU_HEADS = 4
SGU_CHUNK = 128
RET_HEADS = 8
RET_CHUNK = 128
ROPE_BASE = 10000.0
CONV_W = 3
EPS = 1e-6

V7X_VMEM_BYTES = 64 * 1024 * 1024
VMEM_LIMIT_BYTES = V7X_VMEM_BYTES - 4 * 1024 * 1024
SUBLANES = 8
LANES = 128
BF16_SUBLANES = 16

TM = 512
TM_PROJ = 1088
TM_FFN = 1024
TN_FFN = 512
TM_OUT = 256
TM_DOWN = 256
KC_OUT = 128


def _cparams(n_axes):
    return pltpu.CompilerParams(
        dimension_semantics=("arbitrary",) * n_axes, vmem_limit_bytes=VMEM_LIMIT_BYTES)


def _resident(shape):
    zeros = (0,) * len(shape)
    return pl.BlockSpec(shape, lambda *_: zeros, pipeline_mode=pl.Buffered(1))


def _rms(x, g):
    return x * lax.rsqrt(jnp.mean(x * x, axis=-1, keepdims=True) + EPS) * g


def _layer_norm(x, g):
    mu = jnp.mean(x, axis=-1, keepdims=True)
    xc = x - mu
    return xc * lax.rsqrt(jnp.mean(xc * xc, axis=-1, keepdims=True) + EPS) * g


def _first_spec(tm, d, n_first):
    return pl.BlockSpec((tm, d), lambda i: (jnp.minimum(i, n_first - 1), 0))


def _second_spec(tm, d, n_first):
    return pl.BlockSpec((tm, d), lambda i: (jnp.maximum(i - n_first, 0), 0))


def _norm_kernel(xp_ref, xs_ref, g_ref, h_ref, *, n_first):
    i = pl.program_id(0)

    @pl.when(i < n_first)
    def _():
        h_ref[...] = _rms(xp_ref[...], g_ref[...]).astype(BF16)

    @pl.when(i >= n_first)
    def _():
        h_ref[...] = _rms(xs_ref[...], g_ref[...]).astype(BF16)


def _norm(xp, xs, g):
    d = xp.shape[1]
    m = xp.shape[0] + xs.shape[0]
    n_first = xp.shape[0] // TM
    return pl.pallas_call(
        functools.partial(_norm_kernel, n_first=n_first),
        grid=(m // TM,),
        in_specs=[_first_spec(TM, d, n_first), _second_spec(TM, d, n_first), _resident((1, d))],
        out_specs=pl.BlockSpec((TM, d), lambda i: (i, 0)),
        out_shape=jax.ShapeDtypeStruct((m, d), BF16),
        compiler_params=_cparams(1),
        name="norm",
    )(xp, xs, g.reshape(1, d))


def _proj_kernel(h_ref, w_ref, *rest, epilogue, scale):
    *extras, o_ref, wb_ref = rest

    @pl.when(pl.program_id(1) == 0)
    def _():
        wb_ref[...] = w_ref[...].astype(BF16)

    z = jnp.dot(h_ref[...], wb_ref[...], preferred_element_type=F32)
    if epilogue == "none":
        o_ref[...] = z.astype(o_ref.dtype)
    elif epilogue == "gelu":
        o_ref[...] = jax.nn.gelu(z, approximate=True).astype(o_ref.dtype)
    elif epilogue == "silu":
        o_ref[...] = jax.nn.silu(z).astype(o_ref.dtype)
    elif epilogue == "gelu_ln":
        g_ref, b_ref = extras
        v = jax.nn.gelu(z, approximate=True)
        o_ref[...] = (_layer_norm(v, g_ref[...]) + b_ref[...]).astype(o_ref.dtype)
    elif epilogue == "rotary":
        cos_ref, sin_ref = extras
        c = cos_ref[...]
        s = sin_ref[...]
        half = c.shape[-1]
        for hd in range(z.shape[-1] // (2 * half)):
            lo = hd * 2 * half
            x1 = z[:, lo:lo + half]
            x2 = z[:, lo + half:lo + 2 * half]
            o_ref[:, lo:lo + half] = ((x1 * c - x2 * s) * scale).astype(o_ref.dtype)
            o_ref[:, lo + half:lo + 2 * half] = ((x1 * s + x2 * c) * scale).astype(o_ref.dtype)
    else:
        raise ValueError(epilogue)


def _proj(h, w, *, layer, col0, n_out, tn, out_dtype, epilogue, extras=(), extra_specs=(), scale=1.0, name):
    m, k = h.shape
    assert m % TM_PROJ == 0 and n_out % tn == 0
    return pl.pallas_call(
        functools.partial(_proj_kernel, epilogue=epilogue, scale=scale),
        grid=(n_out // tn, m // TM_PROJ),
        in_specs=[pl.BlockSpec((TM_PROJ, k), lambda j, i: (i, 0)),
                  pl.BlockSpec((None, k, tn), lambda j, i: (layer, 0, col0 + j)),
                  *extra_specs],
        out_specs=pl.BlockSpec((TM_PROJ, tn), lambda j, i: (i, j)),
        out_shape=jax.ShapeDtypeStruct((m, n_out), out_dtype),
        scratch_shapes=[pltpu.VMEM((k, tn), BF16)],
        compiler_params=_cparams(2),
        name=name,
    )(h, w, *extras)


def _load_weight_bf16(w_hbm_ref, layer, wb_ref, stage_ref, sem_ref):
    kc = stage_ref.shape[1]
    n_chunks = wb_ref.shape[0] // kc

    def w_copy(c):
        slot = c % 2
        return pltpu.make_async_copy(w_hbm_ref.at[layer, pl.ds(c * kc, kc), :], stage_ref.at[slot], sem_ref.at[slot])

    w_copy(0).start()
    for c in range(n_chunks):
        if c + 1 < n_chunks:
            w_copy(c + 1).start()
        w_copy(c).wait()
        wb_ref[c * kc:(c + 1) * kc, :] = stage_ref[c % 2].astype(BF16)


def _out_kernel(*refs, layer, n_first, split_lhs, split_x, last):
    refs = list(refs)
    wb_ref, stage_ref, sem_ref = refs[-3:]
    lhs_refs = [refs.pop(0) for _ in range(2 if split_lhs else 1)]
    w_hbm_ref = refs.pop(0)
    x_refs = [refs.pop(0) for _ in range(2 if split_x else 1)]
    gpost_ref, gnext_ref, *outs = refs[:-3]
    i = pl.program_id(0)

    def pick(pair):
        return jnp.where(i < n_first, pair[0][...], pair[1][...]) if len(pair) == 2 else pair[0][...]

    @pl.when(i == 0)
    def _():
        _load_weight_bf16(w_hbm_ref, layer, wb_ref, stage_ref, sem_ref)

    y = jnp.dot(pick(lhs_refs), wb_ref[...], preferred_element_type=F32)
    x_new = pick(x_refs) + _rms(y, gpost_ref[...])
    if last:
        yp_ref, ys_ref = outs

        @pl.when(i < n_first)
        def _():
            yp_ref[...] = x_new

        @pl.when(i >= n_first)
        def _():
            ys_ref[...] = x_new
    else:
        xo_ref, h_ref = outs
        xo_ref[...] = x_new
        h_ref[...] = _rms(x_new, gnext_ref[...]).astype(BF16)


def _out_proj(lhs, w_all, x, g_post, g_next, *, layer, tm, m_first, name):
    k, d = w_all.shape[1:]
    n_first = m_first // tm
    split_lhs, split_x = isinstance(lhs, tuple), isinstance(x, tuple)
    m = sum(t.shape[0] for t in lhs) if split_lhs else lhs.shape[0]
    last = g_next is None

    def pair(width):
        return [_first_spec(tm, width, n_first), _second_spec(tm, width, n_first)]

    def operand(t, width):
        return (list(t), pair(width)) if isinstance(t, tuple) else ([t], [pl.BlockSpec((tm, width), lambda i: (i, 0))])

    lhs_args, lhs_specs = operand(lhs, k)
    x_args, x_specs = operand(x, d)
    if last:
        g_next = g_post
        out_specs = pair(d)
        out_shape = [jax.ShapeDtypeStruct((m_first, d), F32), jax.ShapeDtypeStruct((m - m_first, d), F32)]
    else:
        out_specs = [pl.BlockSpec((tm, d), lambda i: (i, 0))] * 2
        out_shape = [jax.ShapeDtypeStruct((m, d), F32), jax.ShapeDtypeStruct((m, d), BF16)]
    res = pl.pallas_call(
        functools.partial(_out_kernel, layer=layer, n_first=n_first, split_lhs=split_lhs, split_x=split_x, last=last),
        grid=(m // tm,),
        in_specs=[*lhs_specs, pl.BlockSpec(memory_space=pl.ANY), *x_specs, _resident((1, d)), _resident((1, d))],
        out_specs=out_specs,
        out_shape=out_shape,
        scratch_shapes=[pltpu.VMEM((k, d), BF16), pltpu.VMEM((2, KC_OUT, d), F32), pltpu.SemaphoreType.DMA((2,))],
        compiler_params=_cparams(1),
        name=name,
    )(*lhs_args, w_all, *x_args, g_post.reshape(1, d), g_next.reshape(1, d))
    return ((res[0], res[1]), None) if last else (res[0], res[1])


def _mix_group(a_ref, u_ref, vn_ref, wgrp_ref, pscale_ref, wmix_ref, bias_ref, o_ref, ext_ref,
               inv_cnt, halo_rows, shift, chunk):
    tm, a_width = a_ref.shape
    gdim = a_width // len(POOL_WINDOWS)
    ext_ref[halo_rows:halo_rows + tm, :] = a_ref[...]
    for gi, w in enumerate(POOL_WINDOWS):
        c0, c1 = gi * gdim, (gi + 1) * gdim
        s = ext_ref[halo_rows:halo_rows + tm, c0:c1]
        for j in range(1, w):
            s = s + ext_ref[halo_rows - j * shift:halo_rows - j * shift + tm, c0:c1]
        d = (s * inv_cnt(gi) - a_ref[:, c0:c1]).astype(BF16)
        z = jnp.dot(d, wgrp_ref[gi], preferred_element_type=F32)
        o_ref[:, c0:c1] = (z * pscale_ref[:, c0:c1]).astype(o_ref.dtype)
    hdim = vn_ref.shape[1] // SGU_HEADS
    for c in range(tm // chunk):
        r0, r1 = c * chunk, (c + 1) * chunk
        for hd in range(SGU_HEADS):
            c0, c1 = hd * hdim, (hd + 1) * hdim
            mixed = jnp.dot(wmix_ref[hd], vn_ref[r0:r1, c0:c1].astype(BF16),
                            preferred_element_type=F32) + bias_ref[hd]
            o_ref[r0:r1, a_width + c0:a_width + c1] = (
                u_ref[r0:r1, c0:c1].astype(F32) * mixed).astype(o_ref.dtype)


def _mix_kernel(a_ref, u_ref, vn_ref, invc_ref, halo_s_ref, wgrp_ref, pscale_ref,
                wmix_p_ref, bias_p_ref, wmix_s_ref, bias_s_ref, o_ref, ext_ref,
                *, n_prompt_tiles, tiles_per_seq, dec_batch):
    i = pl.program_id(0)
    tm = a_ref.shape[0]
    gl = LANES
    halo_p = 2 * SUBLANES

    @pl.when(i < n_prompt_tiles)
    def _prompt():
        @pl.when(i % tiles_per_seq == 0)
        def _():
            ext_ref[0:halo_p, :] = jnp.zeros((halo_p, ext_ref.shape[1]), F32)

        def inv_cnt(gi):
            blk = invc_ref[:, gi * gl:(gi + 1) * gl]
            return jnp.concatenate([blk, blk], axis=1)

        _mix_group(a_ref, u_ref, vn_ref, wgrp_ref, pscale_ref, wmix_p_ref, bias_p_ref, o_ref, ext_ref,
                   inv_cnt, halo_p, 1, SGU_CHUNK)
        ext_ref[0:halo_p, :] = ext_ref[tm:tm + halo_p, :]

    @pl.when(i >= n_prompt_tiles)
    def _sample():
        halo_s = POOL_BUF * dec_batch
        ext_ref[0:halo_s, :] = halo_s_ref[...]
        _mix_group(a_ref, u_ref, vn_ref, wgrp_ref, pscale_ref, wmix_s_ref, bias_s_ref, o_ref, ext_ref,
                   lambda gi: 1.0 / POOL_WINDOWS[gi], halo_s, dec_batch, tm)


def _mix(a, u, vn, invc, halo_s, wgrp, pscale, wmix_p, bias_p, wmix_s, bias_s, *, n_prompt_tiles, tiles_per_seq,
         dec_batch):
    m, a_width = a.shape
    b_width = u.shape[1]
    row = lambda i: (i, 0)
    ext_rows = max(2 * SUBLANES, POOL_BUF * dec_batch) + TM
    return pl.pallas_call(
        functools.partial(_mix_kernel, n_prompt_tiles=n_prompt_tiles, tiles_per_seq=tiles_per_seq,
                          dec_batch=dec_batch),
        grid=(m // TM,),
        in_specs=[pl.BlockSpec((TM, a_width), row), pl.BlockSpec((TM, b_width), row),
                  pl.BlockSpec((TM, b_width), row),
                  pl.BlockSpec((TM, invc.shape[1]), lambda i: (jnp.minimum(i, n_prompt_tiles - 1) % tiles_per_seq, 0)),
                  _resident(halo_s.shape), _resident(wgrp.shape), _resident(pscale.shape),
                  _resident(wmix_p.shape), _resident(bias_p.shape), _resident(wmix_s.shape),
                  _resident(bias_s.shape)],
        out_specs=pl.BlockSpec((TM, a_width + b_width), row),
        out_shape=jax.ShapeDtypeStruct((m, a_width + b_width), BF16),
        scratch_shapes=[pltpu.VMEM((ext_rows, a_width), F32)],
        compiler_params=_cparams(1),
        name="mix",
    )(a, u, vn, invc, halo_s, wgrp, pscale, wmix_p, bias_p, wmix_s, bias_s)


def _ffn_in_kernel(h_ref, wg_ref, wu_ref, wc_ref, bc_ref, cstate_ref, *rest, shift, tiles_per_seq):
    act_ref, tail_ref, wgb_ref, wub_ref, ext_ref = rest
    i = pl.program_id(1)
    tm = h_ref.shape[0]
    halo = cstate_ref.shape[0]

    @pl.when(i == 0)
    def _():
        wgb_ref[...] = wg_ref[...].astype(BF16)
        wub_ref[...] = wu_ref[...].astype(BF16)
        ext_ref[0:halo, :] = cstate_ref[...]

    h = h_ref[...]
    gate = jnp.dot(h, wgb_ref[...], preferred_element_type=F32)
    up = jnp.dot(h, wub_ref[...], preferred_element_type=F32)
    if tiles_per_seq > 1:
        seq_start = (i % tiles_per_seq) == 0
        ext_ref[0:halo, :] = jnp.where(seq_start, cstate_ref[...], ext_ref[0:halo, :])
    ext_ref[halo:halo + tm, :] = gate
    conv = bc_ref[...] + ext_ref[halo - 2 * shift:halo - 2 * shift + tm, :] * wc_ref[0:1, :]
    conv = conv + ext_ref[halo - shift:halo - shift + tm, :] * wc_ref[1:2, :]
    conv = conv + gate * wc_ref[2:3, :]
    act_ref[...] = (jax.nn.gelu(conv, approximate=True) * up).astype(act_ref.dtype)
    tail = gate[tm - halo:tm, :]
    tail_ref[...] = tail
    if tiles_per_seq > 1:
        ext_ref[0:halo, :] = tail


def _ffn_in(h, wg, wu, wc, bc, cstate, *, layer, tm, tn, row0, n_tiles, tiles_per_seq, shift, name):
    m, k = h.shape
    n = wg.shape[2]
    halo = cstate.shape[0]
    wspec = pl.BlockSpec((None, k, tn), lambda j, i: (layer, 0, j))
    in_specs = [pl.BlockSpec((tm, k), lambda j, i: (row0 + i, 0)), wspec, wspec,
                pl.BlockSpec((None, CONV_W, tn), lambda j, i: (layer, 0, j)),
                pl.BlockSpec((None, 1, tn), lambda j, i: (layer, 0, j)),
                pl.BlockSpec((halo, tn), lambda j, i: (0, j))]
    return pl.pallas_call(
        functools.partial(_ffn_in_kernel, shift=shift, tiles_per_seq=tiles_per_seq),
        grid=(n // tn, n_tiles),
        in_specs=in_specs,
        out_specs=[pl.BlockSpec((tm, tn), lambda j, i: (i, j)),
                   pl.BlockSpec((halo, tn), lambda j, i: (i, j))],
        out_shape=[jax.ShapeDtypeStruct((n_tiles * tm, n), BF16),
                   jax.ShapeDtypeStruct((n_tiles * halo, n), F32)],
        scratch_shapes=[pltpu.VMEM((k, tn), BF16), pltpu.VMEM((k, tn), BF16), pltpu.VMEM((halo + tm, tn), F32)],
        compiler_params=_cparams(2),
        name=name,
    )(h, wg, wu, wc, bc, cstate)


def _ret_head(hd, q_ref, k_ref, v_ref, g_ref, s_in_ref, s_out_ref, dmask_ref, xi_ref, zeta_ref, gc_ref, gn_ref):
    dk = q_ref.shape[1] // RET_HEADS
    dv = v_ref.shape[1] // RET_HEADS
    qh = q_ref[:, hd * dk:(hd + 1) * dk]
    kh = k_ref[:, hd * dk:(hd + 1) * dk]
    vh = v_ref[:, hd * dv:(hd + 1) * dv]
    state = s_in_ref[0, hd]
    sc = lax.dot_general(qh, kh, (((1,), (1,)), ((), ())), preferred_element_type=F32) * dmask_ref[hd]
    o = jnp.dot(sc.astype(BF16), vh, preferred_element_type=F32)
    o = o + jnp.dot(qh, state.astype(BF16), preferred_element_type=F32) * xi_ref[hd]
    kz = (kh.astype(F32) * zeta_ref[hd]).astype(BF16)
    s_out_ref[0, hd] = gc_ref[hd] * state + lax.dot_general(
        kz, vh, (((0,), (0,)), ((), ())), preferred_element_type=F32)
    on = _layer_norm(o, gn_ref[hd])
    return (g_ref[:, hd * dv:(hd + 1) * dv].astype(F32) * on).astype(BF16)


def _ret_kernel(qp_ref, kp_ref, vp_ref, gp_ref, qs_ref, ks_ref, vs_ref, gs_ref, s0s_ref,
                dmask_p_ref, xi_p_ref, zeta_p_ref, gc_p_ref, dmask_s_ref, xi_s_ref, zeta_s_ref, gc_s_ref, gn_ref,
                *rest):
    op_ref, os_ref, sp_ref, ss_ref = rest[-4:]

    @pl.when(pl.program_id(1) == 0)
    def _():
        sp_ref[...] = jnp.zeros(sp_ref.shape, F32)

    dv = vp_ref.shape[1] // RET_HEADS
    n_s = s0s_ref.shape[0]
    rows_s = qs_ref.shape[0] // n_s
    for hd in range(RET_HEADS):
        cols = slice(hd * dv, (hd + 1) * dv)
        op_ref[:, cols] = _ret_head(hd, qp_ref, kp_ref, vp_ref, gp_ref, sp_ref, sp_ref,
                                    dmask_p_ref, xi_p_ref, zeta_p_ref, gc_p_ref, gn_ref)
        for j in range(n_s):
            rows = pl.ds(j * rows_s, rows_s)
            os_ref[j * rows_s:(j + 1) * rows_s, cols] = _ret_head(
                hd, qs_ref.at[rows], ks_ref.at[rows], vs_ref.at[rows], gs_ref.at[rows], s0s_ref.at[j:j + 1],
                ss_ref.at[j:j + 1], dmask_s_ref, xi_s_ref, zeta_s_ref, gc_s_ref, gn_ref)


def _retention(qkvg_p, qkvg_s, s0_s_all, sp_prev, ss_prev, tabs_p, tabs_s, gn, *, layer, n_layers, n_seq_p, n_chunk,
               chunk, n_seq_s, rows_s):
    steps = n_seq_p * n_chunk
    per_step = n_seq_s // steps
    assert per_step * steps == n_seq_s
    dq, dvv = qkvg_p[0].shape[1], qkvg_p[2].shape[1]
    state_shape = (RET_HEADS, dq // RET_HEADS, dvv // RET_HEADS)
    blk_p = lambda b, c: (b * n_chunk + c, 0)
    blk_s = lambda b, c: (b * n_chunk + c, 0)
    sp_spec = pl.BlockSpec((None, 1) + state_shape, lambda b, c: (layer, b, 0, 0, 0))
    ss_spec = pl.BlockSpec((None, per_step) + state_shape, lambda b, c: (layer, b * n_chunk + c, 0, 0, 0))
    in_specs = [pl.BlockSpec((chunk, t.shape[1]), blk_p) for t in qkvg_p]
    in_specs += [pl.BlockSpec((per_step * rows_s, t.shape[1]), blk_s) for t in qkvg_s]
    in_specs.append(ss_spec)
    args = [*qkvg_p, *qkvg_s, s0_s_all]
    for t in (*tabs_p, *tabs_s, gn):
        in_specs.append(_resident(t.shape))
        args.append(t)
    aliases = {}
    for prev, out_idx in ((sp_prev, 2), (ss_prev, 3)):
        if prev is not None:
            in_specs.append(pl.BlockSpec(memory_space=pl.ANY))
            args.append(prev)
            aliases[len(args) - 1] = out_idx
    return pl.pallas_call(
        _ret_kernel,
        grid=(n_seq_p, n_chunk),
        in_specs=in_specs,
        out_specs=[pl.BlockSpec((chunk, dvv), blk_p), pl.BlockSpec((per_step * rows_s, dvv), blk_s), sp_spec, ss_spec],
        out_shape=[jax.ShapeDtypeStruct((steps * chunk, dvv), BF16),
                   jax.ShapeDtypeStruct((n_seq_s * rows_s, dvv), BF16),
                   jax.ShapeDtypeStruct((n_layers, n_seq_p) + state_shape, F32),
                   jax.ShapeDtypeStruct((n_layers, n_seq_s) + state_shape, F32)],
        input_output_aliases=aliases,
        compiler_params=_cparams(2),
        name="retention",
    )(*args)


def _decay_tables(c_true, c_pad, dk, dv):
    f32 = np.float32
    lg = np.log1p(-np.exp2(f32(-5.0) - np.arange(RET_HEADS, dtype=f32)))
    idx = np.arange(c_true, dtype=f32)
    diff = idx[:, None] - idx[None, :]
    dmask = np.where(diff >= 0, np.exp(lg[:, None, None] * np.maximum(diff, f32(0.0))), f32(0.0))
    xi = np.exp(lg[:, None] * (idx + f32(1.0)))
    zeta = np.exp(lg[:, None] * (f32(c_true - 1.0) - idx))
    g_c = np.exp(lg * f32(c_true))
    pad = c_pad - c_true
    dmask = np.pad(dmask, ((0, 0), (0, pad), (0, pad)))
    xi = np.pad(xi, ((0, 0), (0, pad)))
    zeta = np.pad(zeta, ((0, 0), (0, pad)))
    tabs = (dmask,
            np.broadcast_to(xi[:, :, None], (RET_HEADS, c_pad, dv)),
            np.broadcast_to(zeta[:, :, None], (RET_HEADS, c_pad, dk)),
            np.broadcast_to(g_c[:, None, None], (RET_HEADS, 1, dv)))
    return tuple(jnp.asarray(t, dtype=F32) for t in tabs)


def kernel(x_prompt, x_sample, state_pool, state_ret, state_conv, w_mix_in, w_pool_grp, pool_scale, w_spatial,
           b_spatial, sgu_norm_g, sgu_norm_b, w_mix_out, w_q, w_k, w_v, w_g, ret_norm_g, w_ret_out, norm_mix_pre,
           norm_mix_post, norm_ffn_pre, norm_ffn_post, w_ffn_gate, w_ffn_up, w_dconv, b_dconv, w_ffn_down):
    bp, seq, d = x_prompt.shape
    bs, dec_seq, _ = x_sample.shape
    depth = norm_mix_pre.shape[0]
    n_ret = w_q.shape[0]
    a_width = w_pool_grp.shape[1] * w_pool_grp.shape[2]
    b_width = sgu_norm_g.shape[1]
    dk = w_q.shape[2] // RET_HEADS
    dv = w_v.shape[2] // RET_HEADS
    d_ff = w_ffn_gate.shape[2]
    m_p, m_s = bp * seq, bs * dec_seq
    m = m_p + m_s
    assert m_s == TM and seq % TM_FFN == 0 and seq % RET_CHUNK == 0 and seq >= POOL_BUF and a_width == b_width
    assert CONV_W - 1 <= dec_seq < POOL_BUF and dec_seq <= SGU_CHUNK and dec_seq <= BF16_SUBLANES
    n_prompt_tiles = m_p // TM
    tiles_per_seq = seq // TM
    geom = dict(n_prompt_tiles=n_prompt_tiles, tiles_per_seq=tiles_per_seq, dec_batch=bs)

    def to_rows(t):
        return t.transpose(1, 0, 2).reshape(t.shape[1] * bs, t.shape[-1])

    def from_rows(r, steps):
        return r.reshape(steps, bs, r.shape[-1]).transpose(1, 0, 2)

    x = (x_prompt.reshape(m_p, d), to_rows(x_sample))

    half = dk // 2
    inv = np.float32(ROPE_BASE) ** (-np.arange(half, dtype=np.float32) / np.float32(half))
    pos_p = np.arange(seq).astype(np.float32)
    pos_s = (PAST_LEN + np.arange(dec_seq)).astype(np.float32)
    ang = np.concatenate([np.tile(pos_p[:, None] * inv[None, :], (bp, 1)),
                          np.repeat(pos_s[:, None] * inv[None, :], bs, axis=0)], axis=0)
    rot = (jnp.asarray(np.cos(ang), dtype=F32), jnp.asarray(np.sin(ang), dtype=F32))
    rot_specs = (pl.BlockSpec((TM_PROJ, half), lambda j, i: (i, 0)),) * 2

    pos = np.arange(seq)
    invc = jnp.asarray(np.concatenate(
        [np.broadcast_to((np.float32(1.0) / np.minimum(pos + 1, w).astype(np.float32))[:, None], (seq, LANES))
         for w in POOL_WINDOWS], axis=1), dtype=F32)

    tril_p = np.tril(np.ones((SGU_CHUNK, SGU_CHUNK), dtype=bool))
    tril_s = np.tril(np.ones((dec_seq, dec_seq), dtype=bool))
    eye_b = np.eye(bs, dtype=np.float32)
    hdim = b_width // SGU_HEADS

    dec_pad = BF16_SUBLANES
    tabs_p = _decay_tables(RET_CHUNK, RET_CHUNK, dk, dv)
    tabs_s = _decay_tables(dec_seq, dec_pad, dk, dv)

    def pad_steps(r):
        t = from_rows(r, dec_seq)
        return jnp.pad(t, ((0, 0), (0, dec_pad - dec_seq), (0, 0))).reshape(bs * dec_pad, r.shape[-1])

    conv0_p = jnp.zeros((SUBLANES, d_ff), F32)
    pool_p, pool_s, vn_s, conv_p, conv_s = [], [], [], [], []
    ret_p = ret_s = None
    h = _norm(*x, norm_mix_pre[0])
    for l in range(depth):
        if l % 2 == 0:
            e = l // 2
            pw = dict(layer=e, tn=a_width, n_out=a_width)
            a = _proj(h, w_mix_in, col0=0, out_dtype=F32, epilogue="none", name="proj_a", **pw)
            u = _proj(h, w_mix_in, col0=1, out_dtype=BF16, epilogue="gelu", name="proj_u", **pw)
            vn = _proj(h, w_mix_in, col0=2, out_dtype=F32, epilogue="gelu_ln",
                       extras=(sgu_norm_g[e].reshape(1, b_width), sgu_norm_b[e].reshape(1, b_width)),
                       extra_specs=(pl.BlockSpec((1, b_width), lambda j, i: (0, 0)),) * 2, name="proj_v", **pw)
            ws_p = jnp.where(tril_p, w_spatial[e][:, :SGU_CHUNK, :SGU_CHUNK], 0.0)
            ws_s = jnp.where(tril_s, w_spatial[e][:, :dec_seq, :dec_seq], 0.0)
            wmix_s = jnp.einsum("hij,bc->hibjc", ws_s, eye_b).reshape(SGU_HEADS, m_s, m_s)
            bias_p = jnp.broadcast_to(b_spatial[e][:, :SGU_CHUNK, None], (SGU_HEADS, SGU_CHUNK, hdim))
            bias_s = jnp.broadcast_to(b_spatial[e][:, :dec_seq, None, None],
                                      (SGU_HEADS, dec_seq, bs, hdim)).reshape(SGU_HEADS, m_s, hdim)
            halo_s = state_pool[e].transpose(1, 0, 2).reshape(POOL_BUF * bs, a_width)
            mixed = _mix(a, u, vn, invc, halo_s, w_pool_grp[e].astype(BF16), pool_scale[e].reshape(1, a_width),
                         ws_p.astype(BF16), bias_p, wmix_s.astype(BF16), bias_s, **geom)
            pool_p.append(jnp.stack([a[(b + 1) * seq - POOL_BUF:(b + 1) * seq] for b in range(bp)]))
            pool_s.append(jnp.concatenate([state_pool[e][:, dec_seq:], from_rows(a[m_p:], dec_seq)], axis=1))
            vn_s.append(from_rows(vn[m_p:], dec_seq))
            w_o = w_mix_out
        else:
            r = l // 2
            pw = dict(layer=r, col0=0, tn=1024, out_dtype=BF16)
            q = _proj(h, w_q, n_out=w_q.shape[2], epilogue="rotary", extras=rot, extra_specs=rot_specs,
                      name="proj_q", **pw)
            k = _proj(h, w_k, n_out=w_k.shape[2], epilogue="rotary", extras=rot, extra_specs=rot_specs,
                      scale=dk ** -0.5, name="proj_k", **pw)
            v = _proj(h, w_v, n_out=w_v.shape[2], epilogue="none", name="proj_v_ret", **pw)
            g = _proj(h, w_g, n_out=w_g.shape[2], epilogue="silu", name="proj_g", **pw)
            gn = ret_norm_g[r].reshape(RET_HEADS, 1, dv)
            qkvg = (q, k, v, g)
            gated, gated_s, ret_p, ret_s = _retention(
                qkvg, tuple(pad_steps(t[m_p:]) for t in qkvg), state_ret, ret_p, ret_s, tabs_p, tabs_s, gn, layer=r,
                n_layers=n_ret, n_seq_p=bp, n_chunk=seq // RET_CHUNK, chunk=RET_CHUNK, n_seq_s=bs, rows_s=dec_pad)
            gated_s = to_rows(gated_s.reshape(bs, dec_pad, -1)[:, :dec_seq])
            mixed, w_o = (gated, gated_s), w_ret_out
        x, h = _out_proj(mixed, w_o, x, norm_mix_post[l], norm_ffn_pre[l], layer=l // 2, tm=TM_OUT, m_first=m_p,
                         name="mix_out")
        ffn = dict(layer=l, tn=TN_FFN)
        wc, bc = w_dconv, b_dconv.reshape(depth, 1, d_ff)
        act_p, tail_p = _ffn_in(h, w_ffn_gate, w_ffn_up, wc, bc, conv0_p, tm=TM_FFN, row0=0, n_tiles=m_p // TM_FFN,
                                tiles_per_seq=seq // TM_FFN, shift=1, name="ffn_in_prompt", **ffn)
        act_s, tail_s = _ffn_in(h, w_ffn_gate, w_ffn_up, wc, bc, to_rows(state_conv[l]), tm=m_s, row0=m_p // m_s,
                                n_tiles=1, tiles_per_seq=1, shift=bs, name="ffn_in_sample", **ffn)
        conv_p.append(tail_p.reshape(bp, seq // TM_FFN, SUBLANES, d_ff)[:, -1, SUBLANES - (CONV_W - 1):])
        conv_s.append(from_rows(tail_s, CONV_W - 1))
        g_next = norm_mix_pre[l + 1] if l + 1 < depth else None
        x, h = _out_proj((act_p, act_s), w_ffn_down, x, norm_ffn_post[l], g_next, layer=l, tm=TM_DOWN, m_first=m_p,
                         name="ffn_out")

    y_prompt = x[0].reshape(bp, seq, d)
    y_sample = from_rows(x[1], dec_seq)
    return (y_prompt, y_sample, jnp.stack(pool_p), jnp.stack(pool_s), jnp.stack(vn_s),
            ret_p, ret_s, jnp.stack(conv_p), jnp.stack(conv_s))
```

```python
import functools

import jax
import jax.numpy as jnp
import numpy as np
from jax import lax
from jax.experimental import pallas as pl
from jax.experimental.pallas import tpu as pltpu

F32 = jnp.float32
BF16 = jnp.bfloat16

PAST_LEN = 16384
POOL_WINDOWS = (2, 4, 8, 16)
POOL_BUF = max(POOL_WINDOWS) - 1
SGU_HEADS = 4
SGU_CHUNK = 128
RET_HEADS = 8
RET_CHUNK = 128
ROPE_BASE = 10000.0
CONV_W = 3
EPS = 1e-6

V7X_VMEM_BYTES = 64 * 1024 * 1024
VMEM_LIMIT_BYTES = V7X_VMEM_BYTES - 4 * 1024 * 1024
SUBLANES = 8
LANES = 128
BF16_SUBLANES = 16

TM = 512
TM_PROJ = 1088
TM_FFN = 1024
TN_FFN = 512
TM_OUT = 256
TM_DOWN = 256
KC_OUT = 256


def _cparams(n_axes):
    return pltpu.CompilerParams(
        dimension_semantics=("arbitrary",) * n_axes, vmem_limit_bytes=VMEM_LIMIT_BYTES)


def _resident(shape):
    zeros = (0,) * len(shape)
    return pl.BlockSpec(shape, lambda *_: zeros, pipeline_mode=pl.Buffered(1))


def _rms(x, g):
    return x * lax.rsqrt(jnp.mean(x * x, axis=-1, keepdims=True) + EPS) * g


def _layer_norm(x, g):
    mu = jnp.mean(x, axis=-1, keepdims=True)
    xc = x - mu
    return xc * lax.rsqrt(jnp.mean(xc * xc, axis=-1, keepdims=True) + EPS) * g


def _first_spec(tm, d, n_first):
    return pl.BlockSpec((tm, d), lambda i: (jnp.minimum(i, n_first - 1), 0))


def _second_spec(tm, d, n_first):
    return pl.BlockSpec((tm, d), lambda i: (jnp.maximum(i - n_first, 0), 0))


def _norm_kernel(xp_ref, xs_ref, g_ref, h_ref, *, n_first):
    i = pl.program_id(0)

    @pl.when(i < n_first)
    def _():
        h_ref[...] = _rms(xp_ref[...], g_ref[...]).astype(BF16)

    @pl.when(i >= n_first)
    def _():
        h_ref[...] = _rms(xs_ref[...], g_ref[...]).astype(BF16)


def _norm(xp, xs, g):
    d = xp.shape[1]
    m = xp.shape[0] + xs.shape[0]
    n_first = xp.shape[0] // TM
    return pl.pallas_call(
        functools.partial(_norm_kernel, n_first=n_first),
        grid=(m // TM,),
        in_specs=[_first_spec(TM, d, n_first), _second_spec(TM, d, n_first), _resident((1, d))],
        out_specs=pl.BlockSpec((TM, d), lambda i: (i, 0)),
        out_shape=jax.ShapeDtypeStruct((m, d), BF16),
        compiler_params=_cparams(1),
        name="norm",
    )(xp, xs, g.reshape(1, d))


def _proj_kernel(h_ref, w_ref, *rest, epilogue, scale):
    *extras, o_ref, wb_ref = rest

    @pl.when(pl.program_id(1) == 0)
    def _():
        wb_ref[...] = w_ref[...].astype(BF16)

    z = jnp.dot(h_ref[...], wb_ref[...], preferred_element_type=F32)
    if epilogue == "none":
        o_ref[...] = z.astype(o_ref.dtype)
    elif epilogue == "gelu":
        o_ref[...] = jax.nn.gelu(z, approximate=True).astype(o_ref.dtype)
    elif epilogue == "silu":
        o_ref[...] = jax.nn.silu(z).astype(o_ref.dtype)
    elif epilogue == "gelu_ln":
        g_ref, b_ref = extras
        v = jax.nn.gelu(z, approximate=True)
        o_ref[...] = (_layer_norm(v, g_ref[...]) + b_ref[...]).astype(o_ref.dtype)
    elif epilogue == "rotary":
        cos_ref, sin_ref = extras
        c = cos_ref[...]
        s = sin_ref[...]
        half = c.shape[-1]
        for hd in range(z.shape[-1] // (2 * half)):
            lo = hd * 2 * half
            x1 = z[:, lo:lo + half]
            x2 = z[:, lo + half:lo + 2 * half]
            o_ref[:, lo:lo + half] = ((x1 * c - x2 * s) * scale).astype(o_ref.dtype)
            o_ref[:, lo + half:lo + 2 * half] = ((x1 * s + x2 * c) * scale).astype(o_ref.dtype)
    else:
        raise ValueError(epilogue)


def _proj(h, w, *, layer, col0, n_out, tn, out_dtype, epilogue, extras=(), extra_specs=(), scale=1.0, name):
    m, k = h.shape
    assert m % TM_PROJ == 0 and n_out % tn == 0
    return pl.pallas_call(
        functools.partial(_proj_kernel, epilogue=epilogue, scale=scale),
        grid=(n_out // tn, m // TM_PROJ),
        in_specs=[pl.BlockSpec((TM_PROJ, k), lambda j, i: (i, 0)),
                  pl.BlockSpec((None, k, tn), lambda j, i: (layer, 0, col0 + j)),
                  *extra_specs],
        out_specs=pl.BlockSpec((TM_PROJ, tn), lambda j, i: (i, j)),
        out_shape=jax.ShapeDtypeStruct((m, n_out), out_dtype),
        scratch_shapes=[pltpu.VMEM((k, tn), BF16)],
        compiler_params=_cparams(2),
        name=name,
    )(h, w, *extras)


def _load_weight_bf16(w_hbm_ref, layer, wb_ref, stage_ref, sem_ref):
    kc = stage_ref.shape[1]
    n_chunks = wb_ref.shape[0] // kc

    def w_copy(c):
        slot = c % 2
        return pltpu.make_async_copy(w_hbm_ref.at[layer, pl.ds(c * kc, kc), :], stage_ref.at[slot], sem_ref.at[slot])

    w_copy(0).start()
    for c in range(n_chunks):
        if c + 1 < n_chunks:
            w_copy(c + 1).start()
        w_copy(c).wait()
        wb_ref[c * kc:(c + 1) * kc, :] = stage_ref[c % 2].astype(BF16)


def _out_kernel(*refs, layer, n_first, split_lhs, split_x, last, w_is_bf16):
    refs = list(refs)
    lhs_refs = [refs.pop(0) for _ in range(2 if split_lhs else 1)]
    w_ref = refs.pop(0)
    x_refs = [refs.pop(0) for _ in range(2 if split_x else 1)]
    i = pl.program_id(0)

    def pick(pair):
        return jnp.where(i < n_first, pair[0][...], pair[1][...]) if len(pair) == 2 else pair[0][...]

    if w_is_bf16:
        wb_ref = w_ref
        gpost_ref, gnext_ref, *outs = refs
    else:
        wb_ref, stage_ref, sem_ref = refs[-3:]
        gpost_ref, gnext_ref, *outs = refs[:-3]

        @pl.when(i == 0)
        def _():
            _load_weight_bf16(w_ref, layer, wb_ref, stage_ref, sem_ref)

    y = jnp.dot(pick(lhs_refs), wb_ref[...], preferred_element_type=F32)
    x_new = pick(x_refs) + _rms(y, gpost_ref[...])
    if last:
        yp_ref, ys_ref = outs

        @pl.when(i < n_first)
        def _():
            yp_ref[...] = x_new

        @pl.when(i >= n_first)
        def _():
            ys_ref[...] = x_new
    else:
        xo_ref, h_ref = outs
        xo_ref[...] = x_new
        h_ref[...] = _rms(x_new, gnext_ref[...]).astype(BF16)


def _out_proj(lhs, w_all, x, g_post, g_next, *, layer, tm, m_first, name):
    w_is_bf16 = w_all.ndim == 2
    k, d = w_all.shape[-2:]
    n_first = m_first // tm
    split_lhs, split_x = isinstance(lhs, tuple), isinstance(x, tuple)
    m = sum(t.shape[0] for t in lhs) if split_lhs else lhs.shape[0]
    last = g_next is None

    def pair(width):
        return [_first_spec(tm, width, n_first), _second_spec(tm, width, n_first)]

    def operand(t, width):
        return (list(t), pair(width)) if isinstance(t, tuple) else ([t], [pl.BlockSpec((tm, width), lambda i: (i, 0))])

    lhs_args, lhs_specs = operand(lhs, k)
    x_args, x_specs = operand(x, d)
    if last:
        g_next = g_post
        out_specs = pair(d)
        out_shape = [jax.ShapeDtypeStruct((m_first, d), F32), jax.ShapeDtypeStruct((m - m_first, d), F32)]
    else:
        out_specs = [pl.BlockSpec((tm, d), lambda i: (i, 0))] * 2
        out_shape = [jax.ShapeDtypeStruct((m, d), F32), jax.ShapeDtypeStruct((m, d), BF16)]
    res = pl.pallas_call(
        functools.partial(_out_kernel, layer=layer, n_first=n_first, split_lhs=split_lhs, split_x=split_x, last=last,
                          w_is_bf16=w_is_bf16),
        grid=(m // tm,),
        in_specs=[*lhs_specs, _resident((k, d)) if w_is_bf16 else pl.BlockSpec(memory_space=pl.ANY), *x_specs,
                  _resident((1, d)), _resident((1, d))],
        out_specs=out_specs,
        out_shape=out_shape,
        scratch_shapes=[] if w_is_bf16 else [pltpu.VMEM((k, d), BF16), pltpu.VMEM((2, KC_OUT, d), F32),
                                             pltpu.SemaphoreType.DMA((2,))],
        compiler_params=_cparams(1),
        name=name,
    )(*lhs_args, w_all, *x_args, g_post.reshape(1, d), g_next.reshape(1, d))
    return ((res[0], res[1]), None) if last else (res[0], res[1])


def _mix_group(a_ref, u_ref, vn_ref, wgrp_ref, pscale_ref, wmix_ref, bias_ref, o_ref, ext_ref,
               inv_cnt, halo_rows, shift, chunk):
    tm, a_width = a_ref.shape
    gdim = a_width // len(POOL_WINDOWS)
    ext_ref[halo_rows:halo_rows + tm, :] = a_ref[...]
    for gi, w in enumerate(POOL_WINDOWS):
        c0, c1 = gi * gdim, (gi + 1) * gdim
        s = ext_ref[halo_rows:halo_rows + tm, c0:c1]
        for j in range(1, w):
            s = s + ext_ref[halo_rows - j * shift:halo_rows - j * shift + tm, c0:c1]
        d = (s * inv_cnt(gi) - a_ref[:, c0:c1]).astype(BF16)
        z = jnp.dot(d, wgrp_ref[gi], preferred_element_type=F32)
        o_ref[:, c0:c1] = (z * pscale_ref[:, c0:c1]).astype(o_ref.dtype)
    hdim = vn_ref.shape[1] // SGU_HEADS
    for c in range(tm // chunk):
        r0, r1 = c * chunk, (c + 1) * chunk
        for hd in range(SGU_HEADS):
            c0, c1 = hd * hdim, (hd + 1) * hdim
            mixed = jnp.dot(wmix_ref[hd], vn_ref[r0:r1, c0:c1].astype(BF16),
                            preferred_element_type=F32) + bias_ref[hd]
            o_ref[r0:r1, a_width + c0:a_width + c1] = (
                u_ref[r0:r1, c0:c1].astype(F32) * mixed).astype(o_ref.dtype)


def _mix_kernel(a_ref, u_ref, vn_ref, invc_ref, halo_s_ref, wgrp_ref, pscale_ref,
                wmix_p_ref, bias_p_ref, wmix_s_ref, bias_s_ref, o_ref, ext_ref,
                *, n_prompt_tiles, tiles_per_seq, dec_batch):
    i = pl.program_id(0)
    tm = a_ref.shape[0]
    gl = LANES
    halo_p = 2 * SUBLANES

    @pl.when(i < n_prompt_tiles)
    def _prompt():
        @pl.when(i % tiles_per_seq == 0)
        def _():
            ext_ref[0:halo_p, :] = jnp.zeros((halo_p, ext_ref.shape[1]), F32)

        def inv_cnt(gi):
            blk = invc_ref[:, gi * gl:(gi + 1) * gl]
            return jnp.concatenate([blk, blk], axis=1)

        _mix_group(a_ref, u_ref, vn_ref, wgrp_ref, pscale_ref, wmix_p_ref, bias_p_ref, o_ref, ext_ref,
                   inv_cnt, halo_p, 1, SGU_CHUNK)
        ext_ref[0:halo_p, :] = ext_ref[tm:tm + halo_p, :]

    @pl.when(i >= n_prompt_tiles)
    def _sample():
        halo_s = POOL_BUF * dec_batch
        ext_ref[0:halo_s, :] = halo_s_ref[...]
        _mix_group(a_ref, u_ref, vn_ref, wgrp_ref, pscale_ref, wmix_s_ref, bias_s_ref, o_ref, ext_ref,
                   lambda gi: 1.0 / POOL_WINDOWS[gi], halo_s, dec_batch, tm)


def _mix(a, u, vn, invc, halo_s, wgrp, pscale, wmix_p, bias_p, wmix_s, bias_s, *, n_prompt_tiles, tiles_per_seq,
         dec_batch):
    m, a_width = a.shape
    b_width = u.shape[1]
    row = lambda i: (i, 0)
    ext_rows = max(2 * SUBLANES, POOL_BUF * dec_batch) + TM
    return pl.pallas_call(
        functools.partial(_mix_kernel, n_prompt_tiles=n_prompt_tiles, tiles_per_seq=tiles_per_seq,
                          dec_batch=dec_batch),
        grid=(m // TM,),
        in_specs=[pl.BlockSpec((TM, a_width), row), pl.BlockSpec((TM, b_width), row),
                  pl.BlockSpec((TM, b_width), row),
                  pl.BlockSpec((TM, invc.shape[1]), lambda i: (jnp.minimum(i, n_prompt_tiles - 1) % tiles_per_seq, 0)),
                  _resident(halo_s.shape), _resident(wgrp.shape), _resident(pscale.shape),
                  _resident(wmix_p.shape), _resident(bias_p.shape), _resident(wmix_s.shape),
                  _resident(bias_s.shape)],
        out_specs=pl.BlockSpec((TM, a_width + b_width), row),
        out_shape=jax.ShapeDtypeStruct((m, a_width + b_width), BF16),
        scratch_shapes=[pltpu.VMEM((ext_rows, a_width), F32)],
        compiler_params=_cparams(1),
        name="mix",
    )(a, u, vn, invc, halo_s, wgrp, pscale, wmix_p, bias_p, wmix_s, bias_s)


def _conv_gelu_gate(gate, prev2, prev1, up, wc_ref, bc_ref):
    conv = bc_ref[...] + prev2 * wc_ref[0:1, :]
    conv = conv + prev1 * wc_ref[1:2, :]
    conv = conv + gate * wc_ref[2:3, :]
    return (jax.nn.gelu(conv, approximate=True) * up).astype(BF16)


def _ffn_in_prompt_kernel(h_ref, wg_ref, wu_ref, wc_ref, bc_ref, cstate_ref, wd_ref, act_ref, tail_ref, wgb_ref,
                          wub_ref, wdb_ref, halo_ref, *, tiles_per_seq):
    i = pl.program_id(1)
    wdb_ref[...] = wd_ref[...].astype(BF16)
    tm = h_ref.shape[0]
    halo = cstate_ref.shape[0]

    @pl.when(i == 0)
    def _():
        wgb_ref[...] = wg_ref[...].astype(BF16)
        wub_ref[...] = wu_ref[...].astype(BF16)
        halo_ref[...] = cstate_ref[...]

    h = h_ref[...]
    gate = jnp.dot(h, wgb_ref[...], preferred_element_type=F32)
    up = jnp.dot(h, wub_ref[...], preferred_element_type=F32)
    before = jnp.where((i % tiles_per_seq) == 0, cstate_ref[...], halo_ref[...])
    row = lax.broadcasted_iota(jnp.int32, before.shape, 0)

    def shifted(s):
        r = pltpu.roll(gate, s, axis=0)
        first = jnp.where(row < s, pltpu.roll(before, s, axis=0), r[0:halo])
        return jnp.concatenate([first, r[halo:]], axis=0)

    act_ref[...] = _conv_gelu_gate(gate, shifted(2), shifted(1), up, wc_ref, bc_ref)
    tail = gate[tm - halo:tm, :]
    tail_ref[...] = tail
    halo_ref[...] = tail


def _ffn_in_sample_kernel(h_ref, wgb_ref, wub_ref, wc_ref, bc_ref, cstate_ref, act_ref, tail_ref, ext_ref, *, shift):
    tm = h_ref.shape[0]
    halo = cstate_ref.shape[0]
    h = h_ref[...]
    gate = jnp.dot(h, wgb_ref[...], preferred_element_type=F32)
    up = jnp.dot(h, wub_ref[...], preferred_element_type=F32)
    ext_ref[0:halo, :] = cstate_ref[...]
    ext_ref[halo:halo + tm, :] = gate
    act_ref[...] = _conv_gelu_gate(gate, ext_ref[halo - 2 * shift:halo - 2 * shift + tm, :],
                                   ext_ref[halo - shift:halo - shift + tm, :], up, wc_ref, bc_ref)
    tail_ref[...] = gate[tm - halo:tm, :]


def _ffn_in_prompt(h, wg, wu, wc, bc, cstate, wd, *, layer, tm, tn, n_tiles, tiles_per_seq):
    k = h.shape[1]
    n = wg.shape[2]
    halo = cstate.shape[0]
    steps = (n // tn) * n_tiles
    wd_rows = wd.shape[1] // steps
    assert wd_rows * steps == wd.shape[1] and wd_rows % BF16_SUBLANES == 0
    d_out = wd.shape[2]
    wspec = pl.BlockSpec((None, k, tn), lambda j, i: (layer, 0, j))
    wbspec = pl.BlockSpec((k, tn), lambda j, i: (0, j))
    return pl.pallas_call(
        functools.partial(_ffn_in_prompt_kernel, tiles_per_seq=tiles_per_seq),
        grid=(n // tn, n_tiles),
        in_specs=[pl.BlockSpec((tm, k), lambda j, i: (i, 0)), wspec, wspec,
                  pl.BlockSpec((None, CONV_W, tn), lambda j, i: (layer, 0, j)),
                  pl.BlockSpec((None, 1, tn), lambda j, i: (layer, 0, j)),
                  pl.BlockSpec((halo, tn), lambda j, i: (0, j)),
                  pl.BlockSpec((None, wd_rows, d_out), lambda j, i: (layer, j * n_tiles + i, 0))],
        out_specs=[pl.BlockSpec((tm, tn), lambda j, i: (i, j)), pl.BlockSpec((halo, tn), lambda j, i: (i, j)),
                   wbspec, wbspec, pl.BlockSpec((wd_rows, d_out), lambda j, i: (j * n_tiles + i, 0))],
        out_shape=[jax.ShapeDtypeStruct((n_tiles * tm, n), BF16), jax.ShapeDtypeStruct((n_tiles * halo, n), F32),
                   jax.ShapeDtypeStruct((k, n), BF16), jax.ShapeDtypeStruct((k, n), BF16),
                   jax.ShapeDtypeStruct(wd.shape[1:], BF16)],
        scratch_shapes=[pltpu.VMEM((halo, tn), F32)],
        compiler_params=_cparams(2),
        name="ffn_in_prompt",
    )(h, wg, wu, wc, bc, cstate, wd)


def _ffn_in_sample(h, wgb, wub, wc, bc, cstate, *, layer, tm, tn, row0, shift):
    k = h.shape[1]
    n = wgb.shape[1]
    halo = cstate.shape[0]
    wbspec = pl.BlockSpec((k, tn), lambda j: (0, j))
    return pl.pallas_call(
        functools.partial(_ffn_in_sample_kernel, shift=shift),
        grid=(n // tn,),
        in_specs=[pl.BlockSpec((tm, k), lambda j: (row0, 0)), wbspec, wbspec,
                  pl.BlockSpec((None, CONV_W, tn), lambda j: (layer, 0, j)),
                  pl.BlockSpec((None, 1, tn), lambda j: (layer, 0, j)),
                  pl.BlockSpec((halo, tn), lambda j: (0, j))],
        out_specs=[pl.BlockSpec((tm, tn), lambda j: (0, j)), pl.BlockSpec((halo, tn), lambda j: (0, j))],
        out_shape=[jax.ShapeDtypeStruct((tm, n), BF16), jax.ShapeDtypeStruct((halo, n), F32)],
        scratch_shapes=[pltpu.VMEM((halo + tm, tn), F32)],
        compiler_params=_cparams(1),
        name="ffn_in_sample",
    )(h, wgb, wub, wc, bc, cstate)


def _ret_head(hd, q_ref, k_ref, v_ref, g_ref, s_in_ref, s_out_ref, dmask_ref, xi_ref, zeta_ref, gc_ref, gn_ref):
    dk = q_ref.shape[1] // RET_HEADS
    dv = v_ref.shape[1] // RET_HEADS
    qh = q_ref[:, hd * dk:(hd + 1) * dk]
    kh = k_ref[:, hd * dk:(hd + 1) * dk]
    vh = v_ref[:, hd * dv:(hd + 1) * dv]
    state = s_in_ref[0, hd]
    sc = lax.dot_general(qh, kh, (((1,), (1,)), ((), ())), preferred_element_type=F32) * dmask_ref[hd]
    o = jnp.dot(sc.astype(BF16), vh, preferred_element_type=F32)
    o = o + jnp.dot(qh, state.astype(BF16), preferred_element_type=F32) * xi_ref[hd]
    kz = (kh.astype(F32) * zeta_ref[hd]).astype(BF16)
    s_out_ref[0, hd] = gc_ref[hd] * state + lax.dot_general(
        kz, vh, (((0,), (0,)), ((), ())), preferred_element_type=F32)
    on = _layer_norm(o, gn_ref[hd])
    return (g_ref[:, hd * dv:(hd + 1) * dv].astype(F32) * on).astype(BF16)


def _ret_kernel(qp_ref, kp_ref, vp_ref, gp_ref, qs_ref, ks_ref, vs_ref, gs_ref, s0s_ref,
                dmask_p_ref, xi_p_ref, zeta_p_ref, gc_p_ref, dmask_s_ref, xi_s_ref, zeta_s_ref, gc_s_ref, gn_ref,
                *rest):
    op_ref, os_ref, sp_ref, ss_ref = rest[-4:]

    @pl.when(pl.program_id(1) == 0)
    def _():
        sp_ref[...] = jnp.zeros(sp_ref.shape, F32)

    dv = vp_ref.shape[1] // RET_HEADS
    n_s = s0s_ref.shape[0]
    rows_s = qs_ref.shape[0] // n_s
    for hd in range(RET_HEADS):
        cols = slice(hd * dv, (hd + 1) * dv)
        op_ref[:, cols] = _ret_head(hd, qp_ref, kp_ref, vp_ref, gp_ref, sp_ref, sp_ref,
                                    dmask_p_ref, xi_p_ref, zeta_p_ref, gc_p_ref, gn_ref)
        for j in range(n_s):
            rows = pl.ds(j * rows_s, rows_s)
            os_ref[j * rows_s:(j + 1) * rows_s, cols] = _ret_head(
                hd, qs_ref.at[rows], ks_ref.at[rows], vs_ref.at[rows], gs_ref.at[rows], s0s_ref.at[j:j + 1],
                ss_ref.at[j:j + 1], dmask_s_ref, xi_s_ref, zeta_s_ref, gc_s_ref, gn_ref)


def _retention(qkvg_p, qkvg_s, s0_s_all, sp_prev, ss_prev, tabs_p, tabs_s, gn, *, layer, n_layers, n_seq_p, n_chunk,
               chunk, n_seq_s, rows_s):
    steps = n_seq_p * n_chunk
    per_step = n_seq_s // steps
    assert per_step * steps == n_seq_s
    dq, dvv = qkvg_p[0].shape[1], qkvg_p[2].shape[1]
    state_shape = (RET_HEADS, dq // RET_HEADS, dvv // RET_HEADS)
    blk_p = lambda b, c: (b * n_chunk + c, 0)
    blk_s = lambda b, c: (b * n_chunk + c, 0)
    sp_spec = pl.BlockSpec((None, 1) + state_shape, lambda b, c: (layer, b, 0, 0, 0))
    ss_spec = pl.BlockSpec((None, per_step) + state_shape, lambda b, c: (layer, b * n_chunk + c, 0, 0, 0))
    in_specs = [pl.BlockSpec((chunk, t.shape[1]), blk_p) for t in qkvg_p]
    in_specs += [pl.BlockSpec((per_step * rows_s, t.shape[1]), blk_s) for t in qkvg_s]
    in_specs.append(ss_spec)
    args = [*qkvg_p, *qkvg_s, s0_s_all]
    for t in (*tabs_p, *tabs_s, gn):
        in_specs.append(_resident(t.shape))
        args.append(t)
    aliases = {}
    for prev, out_idx in ((sp_prev, 2), (ss_prev, 3)):
        if prev is not None:
            in_specs.append(pl.BlockSpec(memory_space=pl.ANY))
            args.append(prev)
            aliases[len(args) - 1] = out_idx
    return pl.pallas_call(
        _ret_kernel,
        grid=(n_seq_p, n_chunk),
        in_specs=in_specs,
        out_specs=[pl.BlockSpec((chunk, dvv), blk_p), pl.BlockSpec((per_step * rows_s, dvv), blk_s), sp_spec, ss_spec],
        out_shape=[jax.ShapeDtypeStruct((steps * chunk, dvv), BF16),
                   jax.ShapeDtypeStruct((n_seq_s * rows_s, dvv), BF16),
                   jax.ShapeDtypeStruct((n_layers, n_seq_p) + state_shape, F32),
                   jax.ShapeDtypeStruct((n_layers, n_seq_s) + state_shape, F32)],
        input_output_aliases=aliases,
        compiler_params=_cparams(2),
        name="retention",
    )(*args)


def _decay_tables(c_true, c_pad, dk, dv):
    f32 = np.float32
    lg = np.log1p(-np.exp2(f32(-5.0) - np.arange(RET_HEADS, dtype=f32)))
    idx = np.arange(c_true, dtype=f32)
    diff = idx[:, None] - idx[None, :]
    dmask = np.where(diff >= 0, np.exp(lg[:, None, None] * np.maximum(diff, f32(0.0))), f32(0.0))
    xi = np.exp(lg[:, None] * (idx + f32(1.0)))
    zeta = np.exp(lg[:, None] * (f32(c_true - 1.0) - idx))
    g_c = np.exp(lg * f32(c_true))
    pad = c_pad - c_true
    dmask = np.pad(dmask, ((0, 0), (0, pad), (0, pad)))
    xi = np.pad(xi, ((0, 0), (0, pad)))
    zeta = np.pad(zeta, ((0, 0), (0, pad)))
    tabs = (dmask,
            np.broadcast_to(xi[:, :, None], (RET_HEADS, c_pad, dv)),
            np.broadcast_to(zeta[:, :, None], (RET_HEADS, c_pad, dk)),
            np.broadcast_to(g_c[:, None, None], (RET_HEADS, 1, dv)))
    return tuple(jnp.asarray(t, dtype=F32) for t in tabs)


def kernel(x_prompt, x_sample, state_pool, state_ret, state_conv, w_mix_in, w_pool_grp, pool_scale, w_spatial,
           b_spatial, sgu_norm_g, sgu_norm_b, w_mix_out, w_q, w_k, w_v, w_g, ret_norm_g, w_ret_out, norm_mix_pre,
           norm_mix_post, norm_ffn_pre, norm_ffn_post, w_ffn_gate, w_ffn_up, w_dconv, b_dconv, w_ffn_down):
    bp, seq, d = x_prompt.shape
    bs, dec_seq, _ = x_sample.shape
    depth = norm_mix_pre.shape[0]
    n_ret = w_q.shape[0]
    a_width = w_pool_grp.shape[1] * w_pool_grp.shape[2]
    b_width = sgu_norm_g.shape[1]
    dk = w_q.shape[2] // RET_HEADS
    dv = w_v.shape[2] // RET_HEADS
    d_ff = w_ffn_gate.shape[2]
    m_p, m_s = bp * seq, bs * dec_seq
    m = m_p + m_s
    assert m_s == TM and seq % TM_FFN == 0 and seq % RET_CHUNK == 0 and seq >= POOL_BUF and a_width == b_width
    assert CONV_W - 1 <= dec_seq < POOL_BUF and dec_seq <= SGU_CHUNK and dec_seq <= BF16_SUBLANES
    n_prompt_tiles = m_p // TM
    tiles_per_seq = seq // TM
    geom = dict(n_prompt_tiles=n_prompt_tiles, tiles_per_seq=tiles_per_seq, dec_batch=bs)

    def to_rows(t):
        return t.transpose(1, 0, 2).reshape(t.shape[1] * bs, t.shape[-1])

    def from_rows(r, steps):
        return r.reshape(steps, bs, r.shape[-1]).transpose(1, 0, 2)

    x = (x_prompt.reshape(m_p, d), to_rows(x_sample))

    half = dk // 2
    inv = np.float32(ROPE_BASE) ** (-np.arange(half, dtype=np.float32) / np.float32(half))
    pos_p = np.arange(seq).astype(np.float32)
    pos_s = (PAST_LEN + np.arange(dec_seq)).astype(np.float32)
    ang = np.concatenate([np.tile(pos_p[:, None] * inv[None, :], (bp, 1)),
                          np.repeat(pos_s[:, None] * inv[None, :], bs, axis=0)], axis=0)
    rot = (jnp.asarray(np.cos(ang), dtype=F32), jnp.asarray(np.sin(ang), dtype=F32))
    rot_specs = (pl.BlockSpec((TM_PROJ, half), lambda j, i: (i, 0)),) * 2

    pos = np.arange(seq)
    invc = jnp.asarray(np.concatenate(
        [np.broadcast_to((np.float32(1.0) / np.minimum(pos + 1, w).astype(np.float32))[:, None], (seq, LANES))
         for w in POOL_WINDOWS], axis=1), dtype=F32)

    tril_p = np.tril(np.ones((SGU_CHUNK, SGU_CHUNK), dtype=bool))
    tril_s = np.tril(np.ones((dec_seq, dec_seq), dtype=bool))
    eye_b = np.eye(bs, dtype=np.float32)
    hdim = b_width // SGU_HEADS

    dec_pad = BF16_SUBLANES
    tabs_p = _decay_tables(RET_CHUNK, RET_CHUNK, dk, dv)
    tabs_s = _decay_tables(dec_seq, dec_pad, dk, dv)

    def pad_steps(r):
        t = from_rows(r, dec_seq)
        return jnp.pad(t, ((0, 0), (0, dec_pad - dec_seq), (0, 0))).reshape(bs * dec_pad, r.shape[-1])

    conv0_p = jnp.zeros((SUBLANES, d_ff), F32)
    pool_p, pool_s, vn_s, conv_p, conv_s = [], [], [], [], []
    ret_p = ret_s = None
    h = _norm(*x, norm_mix_pre[0])
    for l in range(depth):
        if l % 2 == 0:
            e = l // 2
            pw = dict(layer=e, tn=a_width, n_out=a_width)
            a = _proj(h, w_mix_in, col0=0, out_dtype=F32, epilogue="none", name="proj_a", **pw)
            u = _proj(h, w_mix_in, col0=1, out_dtype=BF16, epilogue="gelu", name="proj_u", **pw)
            vn = _proj(h, w_mix_in, col0=2, out_dtype=F32, epilogue="gelu_ln",
                       extras=(sgu_norm_g[e].reshape(1, b_width), sgu_norm_b[e].reshape(1, b_width)),
                       extra_specs=(pl.BlockSpec((1, b_width), lambda j, i: (0, 0)),) * 2, name="proj_v", **pw)
            ws_p = jnp.where(tril_p, w_spatial[e][:, :SGU_CHUNK, :SGU_CHUNK], 0.0)
            ws_s = jnp.where(tril_s, w_spatial[e][:, :dec_seq, :dec_seq], 0.0)
            wmix_s = jnp.einsum("hij,bc->hibjc", ws_s, eye_b).reshape(SGU_HEADS, m_s, m_s)
            bias_p = jnp.broadcast_to(b_spatial[e][:, :SGU_CHUNK, None], (SGU_HEADS, SGU_CHUNK, hdim))
            bias_s = jnp.broadcast_to(b_spatial[e][:, :dec_seq, None, None],
                                      (SGU_HEADS, dec_seq, bs, hdim)).reshape(SGU_HEADS, m_s, hdim)
            halo_s = state_pool[e].transpose(1, 0, 2).reshape(POOL_BUF * bs, a_width)
            mixed = _mix(a, u, vn, invc, halo_s, w_pool_grp[e].astype(BF16), pool_scale[e].reshape(1, a_width),
                         ws_p.astype(BF16), bias_p, wmix_s.astype(BF16), bias_s, **geom)
            pool_p.append(jnp.stack([a[(b + 1) * seq - POOL_BUF:(b + 1) * seq] for b in range(bp)]))
            pool_s.append(jnp.concatenate([state_pool[e][:, dec_seq:], from_rows(a[m_p:], dec_seq)], axis=1))
            vn_s.append(from_rows(vn[m_p:], dec_seq))
            w_o = w_mix_out
        else:
            r = l // 2
            pw = dict(layer=r, col0=0, tn=1024, out_dtype=BF16)
            q = _proj(h, w_q, n_out=w_q.shape[2], epilogue="rotary", extras=rot, extra_specs=rot_specs,
                      name="proj_q", **pw)
            k = _proj(h, w_k, n_out=w_k.shape[2], epilogue="rotary", extras=rot, extra_specs=rot_specs,
                      scale=dk ** -0.5, name="proj_k", **pw)
            v = _proj(h, w_v, n_out=w_v.shape[2], epilogue="none", name="proj_v_ret", **pw)
            g = _proj(h, w_g, n_out=w_g.shape[2], epilogue="silu", name="proj_g", **pw)
            gn = ret_norm_g[r].reshape(RET_HEADS, 1, dv)
            qkvg = (q, k, v, g)
            gated, gated_s, ret_p, ret_s = _retention(
                qkvg, tuple(pad_steps(t[m_p:]) for t in qkvg), state_ret, ret_p, ret_s, tabs_p, tabs_s, gn, layer=r,
                n_layers=n_ret, n_seq_p=bp, n_chunk=seq // RET_CHUNK, chunk=RET_CHUNK, n_seq_s=bs, rows_s=dec_pad)
            gated_s = to_rows(gated_s.reshape(bs, dec_pad, -1)[:, :dec_seq])
            mixed, w_o = (gated, gated_s), w_ret_out
        x, h = _out_proj(mixed, w_o, x, norm_mix_post[l], norm_ffn_pre[l], layer=l // 2, tm=TM_OUT, m_first=m_p,
                         name="mix_out")
        wc, bc = w_dconv, b_dconv.reshape(depth, 1, d_ff)
        act_p, tail_p, wgb, wub, wdb = _ffn_in_prompt(h, w_ffn_gate, w_ffn_up, wc, bc, conv0_p, w_ffn_down, layer=l,
                                                      tm=TM_FFN, tn=TN_FFN, n_tiles=m_p // TM_FFN,
                                                      tiles_per_seq=seq // TM_FFN)
        act_s, tail_s = _ffn_in_sample(h, wgb, wub, wc, bc, to_rows(state_conv[l]), layer=l, tm=m_s, tn=TN_FFN,
                                       row0=m_p // m_s, shift=bs)
        conv_p.append(tail_p.reshape(bp, seq // TM_FFN, SUBLANES, d_ff)[:, -1, SUBLANES - (CONV_W - 1):])
        conv_s.append(from_rows(tail_s, CONV_W - 1))
        g_next = norm_mix_pre[l + 1] if l + 1 < depth else None
        x, h = _out_proj((act_p, act_s), wdb, x, norm_ffn_post[l], g_next, layer=l, tm=TM_DOWN, m_first=m_p,
                         name="ffn_out")

    y_prompt = x[0].reshape(bp, seq, d)
    y_sample = from_rows(x[1], dec_seq)
    return (y_prompt, y_sample, jnp.stack(pool_p), jnp.stack(pool_s), jnp.stack(vn_s),
            ret_p, ret_s, jnp.stack(conv_p), jnp.stack(conv_s))
```

```python
import functools

import jax
import jax.numpy as jnp
import numpy as np
from jax import lax
from jax.experimental import pallas as pl
from jax.experimental.pallas import tpu as pltpu

F32 = jnp.float32
BF16 = jnp.bfloat16

PAST_LEN = 16384
POOL_WINDOWS = (2, 4, 8, 16)
POOL_BUF = max(POOL_WINDOWS) - 1
SGU_HEADS = 4
SGU_CHUNK = 128
RET_HEADS = 8
RET_CHUNK = 128
ROPE_BASE = 10000.0
CONV_W = 3
EPS = 1e-6

V7X_VMEM_BYTES = 64 * 1024 * 1024
VMEM_LIMIT_BYTES = V7X_VMEM_BYTES - 4 * 1024 * 1024
SUBLANES = 8
LANES = 128
BF16_SUBLANES = 16

TM = 512
TM_PROJ = 1088
TM_FFN = 1024
TN_FFN = 512
TM_OUT = 256
TM_DOWN = 256


def _cparams(n_axes):
    return pltpu.CompilerParams(
        dimension_semantics=("arbitrary",) * n_axes, vmem_limit_bytes=VMEM_LIMIT_BYTES)


def _resident(shape):
    zeros = (0,) * len(shape)
    return pl.BlockSpec(shape, lambda *_: zeros, pipeline_mode=pl.Buffered(1))


def _rms(x, g):
    return x * lax.rsqrt(jnp.mean(x * x, axis=-1, keepdims=True) + EPS) * g


def _layer_norm(x, g):
    mu = jnp.mean(x, axis=-1, keepdims=True)
    xc = x - mu
    return xc * lax.rsqrt(jnp.mean(xc * xc, axis=-1, keepdims=True) + EPS) * g


def _first_spec(tm, d, n_first):
    return pl.BlockSpec((tm, d), lambda i: (jnp.minimum(i, n_first - 1), 0))


def _second_spec(tm, d, n_first):
    return pl.BlockSpec((tm, d), lambda i: (jnp.maximum(i - n_first, 0), 0))


def _norm_kernel(xp_ref, xs_ref, g_ref, h_ref, *, n_first):
    i = pl.program_id(0)

    @pl.when(i < n_first)
    def _():
        h_ref[...] = _rms(xp_ref[...], g_ref[...]).astype(BF16)

    @pl.when(i >= n_first)
    def _():
        h_ref[...] = _rms(xs_ref[...], g_ref[...]).astype(BF16)


def _norm(xp, xs, g):
    d = xp.shape[1]
    m = xp.shape[0] + xs.shape[0]
    n_first = xp.shape[0] // TM
    return pl.pallas_call(
        functools.partial(_norm_kernel, n_first=n_first),
        grid=(m // TM,),
        in_specs=[_first_spec(TM, d, n_first), _second_spec(TM, d, n_first), _resident((1, d))],
        out_specs=pl.BlockSpec((TM, d), lambda i: (i, 0)),
        out_shape=jax.ShapeDtypeStruct((m, d), BF16),
        compiler_params=_cparams(1),
        name="norm",
    )(xp, xs, g.reshape(1, d))


def _proj_kernel(h_ref, w_ref, *rest, epilogue, scale, side_cast):
    if side_cast:
        *extras, side_ref, o_ref, side_out_ref, wb_ref = rest
        side_out_ref[...] = side_ref[...].astype(BF16)
    else:
        *extras, o_ref, wb_ref = rest

    @pl.when(pl.program_id(1) == 0)
    def _():
        wb_ref[...] = w_ref[...].astype(BF16)

    z = jnp.dot(h_ref[...], wb_ref[...], preferred_element_type=F32)
    if epilogue == "none":
        o_ref[...] = z.astype(o_ref.dtype)
    elif epilogue == "gelu":
        o_ref[...] = jax.nn.gelu(z, approximate=True).astype(o_ref.dtype)
    elif epilogue == "silu":
        o_ref[...] = jax.nn.silu(z).astype(o_ref.dtype)
    elif epilogue == "gelu_ln":
        g_ref, b_ref = extras
        v = jax.nn.gelu(z, approximate=True)
        o_ref[...] = (_layer_norm(v, g_ref[...]) + b_ref[...]).astype(o_ref.dtype)
    elif epilogue == "rotary":
        cos_ref, sin_ref = extras
        c = cos_ref[...]
        s = sin_ref[...]
        half = c.shape[-1]
        for hd in range(z.shape[-1] // (2 * half)):
            lo = hd * 2 * half
            x1 = z[:, lo:lo + half]
            x2 = z[:, lo + half:lo + 2 * half]
            o_ref[:, lo:lo + half] = ((x1 * c - x2 * s) * scale).astype(o_ref.dtype)
            o_ref[:, lo + half:lo + 2 * half] = ((x1 * s + x2 * c) * scale).astype(o_ref.dtype)
    else:
        raise ValueError(epilogue)


def _proj(h, w, *, layer, col0, n_out, tn, out_dtype, epilogue, extras=(), extra_specs=(), scale=1.0, name,
          side=None):
    m, k = h.shape
    assert m % TM_PROJ == 0 and n_out % tn == 0
    n_i = m // TM_PROJ
    in_specs = [pl.BlockSpec((TM_PROJ, k), lambda j, i: (i, 0)),
                pl.BlockSpec((None, k, tn), lambda j, i: (layer, 0, col0 + j)), *extra_specs]
    args = [h, w, *extras]
    out_specs = [pl.BlockSpec((TM_PROJ, tn), lambda j, i: (i, j))]
    out_shape = [jax.ShapeDtypeStruct((m, n_out), out_dtype)]
    if side is not None:
        w2, layer2 = side
        rows = w2.shape[1] // ((n_out // tn) * n_i)
        assert rows * (n_out // tn) * n_i == w2.shape[1] and rows % BF16_SUBLANES == 0
        in_specs.append(pl.BlockSpec((None, rows, w2.shape[2]), lambda j, i: (layer2, j * n_i + i, 0)))
        args.append(w2)
        out_specs.append(pl.BlockSpec((rows, w2.shape[2]), lambda j, i: (j * n_i + i, 0)))
        out_shape.append(jax.ShapeDtypeStruct(w2.shape[1:], BF16))
    res = pl.pallas_call(
        functools.partial(_proj_kernel, epilogue=epilogue, scale=scale, side_cast=side is not None),
        grid=(n_out // tn, n_i),
        in_specs=in_specs,
        out_specs=out_specs,
        out_shape=out_shape,
        scratch_shapes=[pltpu.VMEM((k, tn), BF16)],
        compiler_params=_cparams(2),
        name=name,
    )(*args)
    return res[0] if side is None else tuple(res)


def _out_kernel(*refs, n_first, split_lhs, split_x, last):
    refs = list(refs)
    lhs_refs = [refs.pop(0) for _ in range(2 if split_lhs else 1)]
    wb_ref = refs.pop(0)
    x_refs = [refs.pop(0) for _ in range(2 if split_x else 1)]
    gpost_ref, gnext_ref, *outs = refs
    i = pl.program_id(0)

    def pick(pair):
        return jnp.where(i < n_first, pair[0][...], pair[1][...]) if len(pair) == 2 else pair[0][...]

    y = jnp.dot(pick(lhs_refs), wb_ref[...], preferred_element_type=F32)
    x_new = pick(x_refs) + _rms(y, gpost_ref[...])
    if last:
        yp_ref, ys_ref = outs

        @pl.when(i < n_first)
        def _():
            yp_ref[...] = x_new

        @pl.when(i >= n_first)
        def _():
            ys_ref[...] = x_new
    else:
        xo_ref, h_ref = outs
        xo_ref[...] = x_new
        h_ref[...] = _rms(x_new, gnext_ref[...]).astype(BF16)


def _out_proj(lhs, w, x, g_post, g_next, *, tm, m_first, name):
    k, d = w.shape
    n_first = m_first // tm
    split_lhs, split_x = isinstance(lhs, tuple), isinstance(x, tuple)
    m = sum(t.shape[0] for t in lhs) if split_lhs else lhs.shape[0]
    last = g_next is None

    def pair(width):
        return [_first_spec(tm, width, n_first), _second_spec(tm, width, n_first)]

    def operand(t, width):
        return (list(t), pair(width)) if isinstance(t, tuple) else ([t], [pl.BlockSpec((tm, width), lambda i: (i, 0))])

    lhs_args, lhs_specs = operand(lhs, k)
    x_args, x_specs = operand(x, d)
    if last:
        g_next = g_post
        out_specs = pair(d)
        out_shape = [jax.ShapeDtypeStruct((m_first, d), F32), jax.ShapeDtypeStruct((m - m_first, d), F32)]
    else:
        out_specs = [pl.BlockSpec((tm, d), lambda i: (i, 0))] * 2
        out_shape = [jax.ShapeDtypeStruct((m, d), F32), jax.ShapeDtypeStruct((m, d), BF16)]
    res = pl.pallas_call(
        functools.partial(_out_kernel, n_first=n_first, split_lhs=split_lhs, split_x=split_x, last=last),
        grid=(m // tm,),
        in_specs=[*lhs_specs, _resident((k, d)), *x_specs, _resident((1, d)), _resident((1, d))],
        out_specs=out_specs,
        out_shape=out_shape,
        compiler_params=_cparams(1),
        name=name,
    )(*lhs_args, w, *x_args, g_post.reshape(1, d), g_next.reshape(1, d))
    return ((res[0], res[1]), None) if last else (res[0], res[1])


def _mix_group(a_ref, u_ref, vn_ref, wgrp_ref, pscale_ref, wmix_ref, bias_ref, o_ref, ext_ref,
               inv_cnt, halo_rows, shift, chunk):
    tm, a_width = a_ref.shape
    gdim = a_width // len(POOL_WINDOWS)
    ext_ref[halo_rows:halo_rows + tm, :] = a_ref[...]
    for gi, w in enumerate(POOL_WINDOWS):
        c0, c1 = gi * gdim, (gi + 1) * gdim
        s = ext_ref[halo_rows:halo_rows + tm, c0:c1]
        for j in range(1, w):
            s = s + ext_ref[halo_rows - j * shift:halo_rows - j * shift + tm, c0:c1]
        d = (s * inv_cnt(gi) - a_ref[:, c0:c1]).astype(BF16)
        z = jnp.dot(d, wgrp_ref[gi], preferred_element_type=F32)
        o_ref[:, c0:c1] = (z * pscale_ref[:, c0:c1]).astype(o_ref.dtype)
    hdim = vn_ref.shape[1] // SGU_HEADS
    for c in range(tm // chunk):
        r0, r1 = c * chunk, (c + 1) * chunk
        for hd in range(SGU_HEADS):
            c0, c1 = hd * hdim, (hd + 1) * hdim
            mixed = jnp.dot(wmix_ref[hd], vn_ref[r0:r1, c0:c1].astype(BF16),
                            preferred_element_type=F32) + bias_ref[hd]
            o_ref[r0:r1, a_width + c0:a_width + c1] = (
                u_ref[r0:r1, c0:c1].astype(F32) * mixed).astype(o_ref.dtype)


def _mix_kernel(a_ref, u_ref, vn_ref, invc_ref, halo_s_ref, wgrp_ref, pscale_ref,
                wmix_p_ref, bias_p_ref, wmix_s_ref, bias_s_ref, o_ref, ext_ref,
                *, n_prompt_tiles, tiles_per_seq, dec_batch):
    i = pl.program_id(0)
    tm = a_ref.shape[0]
    gl = LANES
    halo_p = 2 * SUBLANES

    @pl.when(i < n_prompt_tiles)
    def _prompt():
        @pl.when(i % tiles_per_seq == 0)
        def _():
            ext_ref[0:halo_p, :] = jnp.zeros((halo_p, ext_ref.shape[1]), F32)

        def inv_cnt(gi):
            blk = invc_ref[:, gi * gl:(gi + 1) * gl]
            return jnp.concatenate([blk, blk], axis=1)

        _mix_group(a_ref, u_ref, vn_ref, wgrp_ref, pscale_ref, wmix_p_ref, bias_p_ref, o_ref, ext_ref,
                   inv_cnt, halo_p, 1, SGU_CHUNK)
        ext_ref[0:halo_p, :] = ext_ref[tm:tm + halo_p, :]

    @pl.when(i >= n_prompt_tiles)
    def _sample():
        halo_s = POOL_BUF * dec_batch
        ext_ref[0:halo_s, :] = halo_s_ref[...]
        _mix_group(a_ref, u_ref, vn_ref, wgrp_ref, pscale_ref, wmix_s_ref, bias_s_ref, o_ref, ext_ref,
                   lambda gi: 1.0 / POOL_WINDOWS[gi], halo_s, dec_batch, tm)


def _mix(a, u, vn, invc, halo_s, wgrp, pscale, wmix_p, bias_p, wmix_s, bias_s, *, n_prompt_tiles, tiles_per_seq,
         dec_batch):
    m, a_width = a.shape
    b_width = u.shape[1]
    row = lambda i: (i, 0)
    ext_rows = max(2 * SUBLANES, POOL_BUF * dec_batch) + TM
    return pl.pallas_call(
        functools.partial(_mix_kernel, n_prompt_tiles=n_prompt_tiles, tiles_per_seq=tiles_per_seq,
                          dec_batch=dec_batch),
        grid=(m // TM,),
        in_specs=[pl.BlockSpec((TM, a_width), row), pl.BlockSpec((TM, b_width), row),
                  pl.BlockSpec((TM, b_width), row),
                  pl.BlockSpec((TM, invc.shape[1]), lambda i: (jnp.minimum(i, n_prompt_tiles - 1) % tiles_per_seq, 0)),
                  _resident(halo_s.shape), _resident(wgrp.shape), _resident(pscale.shape),
                  _resident(wmix_p.shape), _resident(bias_p.shape), _resident(wmix_s.shape),
                  _resident(bias_s.shape)],
        out_specs=pl.BlockSpec((TM, a_width + b_width), row),
        out_shape=jax.ShapeDtypeStruct((m, a_width + b_width), BF16),
        scratch_shapes=[pltpu.VMEM((ext_rows, a_width), F32)],
        compiler_params=_cparams(1),
        name="mix",
    )(a, u, vn, invc, halo_s, wgrp, pscale, wmix_p, bias_p, wmix_s, bias_s)


def _conv_gelu_gate(gate, prev2, prev1, up, wc_ref, bc_ref):
    conv = bc_ref[...] + prev2 * wc_ref[0:1, :]
    conv = conv + prev1 * wc_ref[1:2, :]
    conv = conv + gate * wc_ref[2:3, :]
    return (jax.nn.gelu(conv, approximate=True) * up).astype(BF16)


def _ffn_in_prompt_kernel(h_ref, wg_ref, wu_ref, wc_ref, bc_ref, cstate_ref, wd_ref, act_ref, tail_ref, wgb_ref,
                          wub_ref, wdb_ref, halo_ref, *, tiles_per_seq):
    i = pl.program_id(1)
    wdb_ref[...] = wd_ref[...].astype(BF16)
    tm = h_ref.shape[0]
    halo = cstate_ref.shape[0]

    @pl.when(i == 0)
    def _():
        wgb_ref[...] = wg_ref[...].astype(BF16)
        wub_ref[...] = wu_ref[...].astype(BF16)
        halo_ref[...] = cstate_ref[...]

    h = h_ref[...]
    gate = jnp.dot(h, wgb_ref[...], preferred_element_type=F32)
    up = jnp.dot(h, wub_ref[...], preferred_element_type=F32)
    before = jnp.where((i % tiles_per_seq) == 0, cstate_ref[...], halo_ref[...])
    row = lax.broadcasted_iota(jnp.int32, before.shape, 0)

    def shifted(s):
        r = pltpu.roll(gate, s, axis=0)
        first = jnp.where(row < s, pltpu.roll(before, s, axis=0), r[0:halo])
        return jnp.concatenate([first, r[halo:]], axis=0)

    act_ref[...] = _conv_gelu_gate(gate, shifted(2), shifted(1), up, wc_ref, bc_ref)
    tail = gate[tm - halo:tm, :]
    tail_ref[...] = tail
    halo_ref[...] = tail


def _ffn_in_sample_kernel(h_ref, wgb_ref, wub_ref, wc_ref, bc_ref, cstate_ref, act_ref, tail_ref, ext_ref, *, shift):
    tm = h_ref.shape[0]
    halo = cstate_ref.shape[0]
    h = h_ref[...]
    gate = jnp.dot(h, wgb_ref[...], preferred_element_type=F32)
    up = jnp.dot(h, wub_ref[...], preferred_element_type=F32)
    ext_ref[0:halo, :] = cstate_ref[...]
    ext_ref[halo:halo + tm, :] = gate
    act_ref[...] = _conv_gelu_gate(gate, ext_ref[halo - 2 * shift:halo - 2 * shift + tm, :],
                                   ext_ref[halo - shift:halo - shift + tm, :], up, wc_ref, bc_ref)
    tail_ref[...] = gate[tm - halo:tm, :]


def _ffn_in_prompt(h, wg, wu, wc, bc, cstate, wd, *, layer, tm, tn, n_tiles, tiles_per_seq):
    k = h.shape[1]
    n = wg.shape[2]
    halo = cstate.shape[0]
    steps = (n // tn) * n_tiles
    wd_rows = wd.shape[1] // steps
    assert wd_rows * steps == wd.shape[1] and wd_rows % BF16_SUBLANES == 0
    d_out = wd.shape[2]
    wspec = pl.BlockSpec((None, k, tn), lambda j, i: (layer, 0, j))
    wbspec = pl.BlockSpec((k, tn), lambda j, i: (0, j))
    return pl.pallas_call(
        functools.partial(_ffn_in_prompt_kernel, tiles_per_seq=tiles_per_seq),
        grid=(n // tn, n_tiles),
        in_specs=[pl.BlockSpec((tm, k), lambda j, i: (i, 0)), wspec, wspec,
                  pl.BlockSpec((None, CONV_W, tn), lambda j, i: (layer, 0, j)),
                  pl.BlockSpec((None, 1, tn), lambda j, i: (layer, 0, j)),
                  pl.BlockSpec((halo, tn), lambda j, i: (0, j)),
                  pl.BlockSpec((None, wd_rows, d_out), lambda j, i: (layer, j * n_tiles + i, 0))],
        out_specs=[pl.BlockSpec((tm, tn), lambda j, i: (i, j)), pl.BlockSpec((halo, tn), lambda j, i: (i, j)),
                   wbspec, wbspec, pl.BlockSpec((wd_rows, d_out), lambda j, i: (j * n_tiles + i, 0))],
        out_shape=[jax.ShapeDtypeStruct((n_tiles * tm, n), BF16), jax.ShapeDtypeStruct((n_tiles * halo, n), F32),
                   jax.ShapeDtypeStruct((k, n), BF16), jax.ShapeDtypeStruct((k, n), BF16),
                   jax.ShapeDtypeStruct(wd.shape[1:], BF16)],
        scratch_shapes=[pltpu.VMEM((halo, tn), F32)],
        compiler_params=_cparams(2),
        name="ffn_in_prompt",
    )(h, wg, wu, wc, bc, cstate, wd)


def _ffn_in_sample(h, wgb, wub, wc, bc, cstate, *, layer, tm, tn, row0, shift):
    k = h.shape[1]
    n = wgb.shape[1]
    halo = cstate.shape[0]
    wbspec = pl.BlockSpec((k, tn), lambda j: (0, j))
    return pl.pallas_call(
        functools.partial(_ffn_in_sample_kernel, shift=shift),
        grid=(n // tn,),
        in_specs=[pl.BlockSpec((tm, k), lambda j: (row0, 0)), wbspec, wbspec,
                  pl.BlockSpec((None, CONV_W, tn), lambda j: (layer, 0, j)),
                  pl.BlockSpec((None, 1, tn), lambda j: (layer, 0, j)),
                  pl.BlockSpec((halo, tn), lambda j: (0, j))],
        out_specs=[pl.BlockSpec((tm, tn), lambda j: (0, j)), pl.BlockSpec((halo, tn), lambda j: (0, j))],
        out_shape=[jax.ShapeDtypeStruct((tm, n), BF16), jax.ShapeDtypeStruct((halo, n), F32)],
        scratch_shapes=[pltpu.VMEM((halo + tm, tn), F32)],
        compiler_params=_cparams(1),
        name="ffn_in_sample",
    )(h, wgb, wub, wc, bc, cstate)


def _ret_head(hd, q_ref, k_ref, v_ref, g_ref, s_in_ref, s_out_ref, dmask_ref, xi_ref, zeta_ref, gc_ref, gn_ref):
    dk = q_ref.shape[1] // RET_HEADS
    dv = v_ref.shape[1] // RET_HEADS
    qh = q_ref[:, hd * dk:(hd + 1) * dk]
    kh = k_ref[:, hd * dk:(hd + 1) * dk]
    vh = v_ref[:, hd * dv:(hd + 1) * dv]
    state = s_in_ref[0, hd]
    sc = lax.dot_general(qh, kh, (((1,), (1,)), ((), ())), preferred_element_type=F32) * dmask_ref[hd]
    o = jnp.dot(sc.astype(BF16), vh, preferred_element_type=F32)
    o = o + jnp.dot(qh, state.astype(BF16), preferred_element_type=F32) * xi_ref[hd]
    kz = (kh.astype(F32) * zeta_ref[hd]).astype(BF16)
    s_out_ref[0, hd] = gc_ref[hd] * state + lax.dot_general(
        kz, vh, (((0,), (0,)), ((), ())), preferred_element_type=F32)
    on = _layer_norm(o, gn_ref[hd])
    return (g_ref[:, hd * dv:(hd + 1) * dv].astype(F32) * on).astype(BF16)


def _ret_kernel(qp_ref, kp_ref, vp_ref, gp_ref, qs_ref, ks_ref, vs_ref, gs_ref, s0s_ref,
                dmask_p_ref, xi_p_ref, zeta_p_ref, gc_p_ref, dmask_s_ref, xi_s_ref, zeta_s_ref, gc_s_ref, gn_ref,
                *rest):
    op_ref, os_ref, sp_ref, ss_ref = rest[-4:]

    @pl.when(pl.program_id(1) == 0)
    def _():
        sp_ref[...] = jnp.zeros(sp_ref.shape, F32)

    dv = vp_ref.shape[1] // RET_HEADS
    n_s = s0s_ref.shape[0]
    rows_s = qs_ref.shape[0] // n_s
    for hd in range(RET_HEADS):
        cols = slice(hd * dv, (hd + 1) * dv)
        op_ref[:, cols] = _ret_head(hd, qp_ref, kp_ref, vp_ref, gp_ref, sp_ref, sp_ref,
                                    dmask_p_ref, xi_p_ref, zeta_p_ref, gc_p_ref, gn_ref)
        for j in range(n_s):
            rows = pl.ds(j * rows_s, rows_s)
            os_ref[j * rows_s:(j + 1) * rows_s, cols] = _ret_head(
                hd, qs_ref.at[rows], ks_ref.at[rows], vs_ref.at[rows], gs_ref.at[rows], s0s_ref.at[j:j + 1],
                ss_ref.at[j:j + 1], dmask_s_ref, xi_s_ref, zeta_s_ref, gc_s_ref, gn_ref)


def _retention(qkvg_p, qkvg_s, s0_s_all, sp_prev, ss_prev, tabs_p, tabs_s, gn, *, layer, n_layers, n_seq_p, n_chunk,
               chunk, n_seq_s, rows_s):
    steps = n_seq_p * n_chunk
    per_step = n_seq_s // steps
    assert per_step * steps == n_seq_s
    dq, dvv = qkvg_p[0].shape[1], qkvg_p[2].shape[1]
    state_shape = (RET_HEADS, dq // RET_HEADS, dvv // RET_HEADS)
    blk_p = lambda b, c: (b * n_chunk + c, 0)
    blk_s = lambda b, c: (b * n_chunk + c, 0)
    sp_spec = pl.BlockSpec((None, 1) + state_shape, lambda b, c: (layer, b, 0, 0, 0))
    ss_spec = pl.BlockSpec((None, per_step) + state_shape, lambda b, c: (layer, b * n_chunk + c, 0, 0, 0))
    in_specs = [pl.BlockSpec((chunk, t.shape[1]), blk_p) for t in qkvg_p]
    col = 0
    for t in qkvg_p:
        width = t.shape[1]
        assert col % width == 0
        in_specs.append(pl.BlockSpec((per_step * rows_s, width), lambda b, c, cb=col // width: (b * n_chunk + c, cb)))
        col += width
    in_specs.append(ss_spec)
    args = [*qkvg_p, *(qkvg_s,) * len(qkvg_p), s0_s_all]
    for t in (*tabs_p, *tabs_s, gn):
        in_specs.append(_resident(t.shape))
        args.append(t)
    aliases = {}
    for prev, out_idx in ((sp_prev, 2), (ss_prev, 3)):
        if prev is not None:
            in_specs.append(pl.BlockSpec(memory_space=pl.ANY))
            args.append(prev)
            aliases[len(args) - 1] = out_idx
    return pl.pallas_call(
        _ret_kernel,
        grid=(n_seq_p, n_chunk),
        in_specs=in_specs,
        out_specs=[pl.BlockSpec((chunk, dvv), blk_p), pl.BlockSpec((per_step * rows_s, dvv), blk_s), sp_spec, ss_spec],
        out_shape=[jax.ShapeDtypeStruct((steps * chunk, dvv), BF16),
                   jax.ShapeDtypeStruct((n_seq_s * rows_s, dvv), BF16),
                   jax.ShapeDtypeStruct((n_layers, n_seq_p) + state_shape, F32),
                   jax.ShapeDtypeStruct((n_layers, n_seq_s) + state_shape, F32)],
        input_output_aliases=aliases,
        compiler_params=_cparams(2),
        name="retention",
    )(*args)


def _decay_tables(c_true, c_pad, dk, dv):
    f32 = np.float32
    lg = np.log1p(-np.exp2(f32(-5.0) - np.arange(RET_HEADS, dtype=f32)))
    idx = np.arange(c_true, dtype=f32)
    diff = idx[:, None] - idx[None, :]
    dmask = np.where(diff >= 0, np.exp(lg[:, None, None] * np.maximum(diff, f32(0.0))), f32(0.0))
    xi = np.exp(lg[:, None] * (idx + f32(1.0)))
    zeta = np.exp(lg[:, None] * (f32(c_true - 1.0) - idx))
    g_c = np.exp(lg * f32(c_true))
    pad = c_pad - c_true
    dmask = np.pad(dmask, ((0, 0), (0, pad), (0, pad)))
    xi = np.pad(xi, ((0, 0), (0, pad)))
    zeta = np.pad(zeta, ((0, 0), (0, pad)))
    tabs = (dmask,
            np.broadcast_to(xi[:, :, None], (RET_HEADS, c_pad, dv)),
            np.broadcast_to(zeta[:, :, None], (RET_HEADS, c_pad, dk)),
            np.broadcast_to(g_c[:, None, None], (RET_HEADS, 1, dv)))
    return tuple(jnp.asarray(t, dtype=F32) for t in tabs)


def kernel(x_prompt, x_sample, state_pool, state_ret, state_conv, w_mix_in, w_pool_grp, pool_scale, w_spatial,
           b_spatial, sgu_norm_g, sgu_norm_b, w_mix_out, w_q, w_k, w_v, w_g, ret_norm_g, w_ret_out, norm_mix_pre,
           norm_mix_post, norm_ffn_pre, norm_ffn_post, w_ffn_gate, w_ffn_up, w_dconv, b_dconv, w_ffn_down):
    bp, seq, d = x_prompt.shape
    bs, dec_seq, _ = x_sample.shape
    depth = norm_mix_pre.shape[0]
    n_ret = w_q.shape[0]
    a_width = w_pool_grp.shape[1] * w_pool_grp.shape[2]
    b_width = sgu_norm_g.shape[1]
    dk = w_q.shape[2] // RET_HEADS
    dv = w_v.shape[2] // RET_HEADS
    d_ff = w_ffn_gate.shape[2]
    m_p, m_s = bp * seq, bs * dec_seq
    m = m_p + m_s
    assert m_s == TM and seq % TM_FFN == 0 and seq % RET_CHUNK == 0 and seq >= POOL_BUF and a_width == b_width
    assert CONV_W - 1 <= dec_seq < POOL_BUF and dec_seq <= SGU_CHUNK and dec_seq <= BF16_SUBLANES
    n_prompt_tiles = m_p // TM
    tiles_per_seq = seq // TM
    geom = dict(n_prompt_tiles=n_prompt_tiles, tiles_per_seq=tiles_per_seq, dec_batch=bs)

    def to_rows(t):
        return t.transpose(1, 0, 2).reshape(t.shape[1] * bs, t.shape[-1])

    def from_rows(r, steps):
        return r.reshape(steps, bs, r.shape[-1]).transpose(1, 0, 2)

    x = (x_prompt.reshape(m_p, d), to_rows(x_sample))

    half = dk // 2
    inv = np.float32(ROPE_BASE) ** (-np.arange(half, dtype=np.float32) / np.float32(half))
    pos_p = np.arange(seq).astype(np.float32)
    pos_s = (PAST_LEN + np.arange(dec_seq)).astype(np.float32)
    ang = np.concatenate([np.tile(pos_p[:, None] * inv[None, :], (bp, 1)),
                          np.repeat(pos_s[:, None] * inv[None, :], bs, axis=0)], axis=0)
    rot = (jnp.asarray(np.cos(ang), dtype=F32), jnp.asarray(np.sin(ang), dtype=F32))
    rot_specs = (pl.BlockSpec((TM_PROJ, half), lambda j, i: (i, 0)),) * 2

    pos = np.arange(seq)
    invc = jnp.asarray(np.concatenate(
        [np.broadcast_to((np.float32(1.0) / np.minimum(pos + 1, w).astype(np.float32))[:, None], (seq, LANES))
         for w in POOL_WINDOWS], axis=1), dtype=F32)

    tril_p = np.tril(np.ones((SGU_CHUNK, SGU_CHUNK), dtype=bool))
    tril_s = np.tril(np.ones((dec_seq, dec_seq), dtype=bool))
    eye_b = np.eye(bs, dtype=np.float32)
    hdim = b_width // SGU_HEADS

    dec_pad = BF16_SUBLANES
    tabs_p = _decay_tables(RET_CHUNK, RET_CHUNK, dk, dv)
    tabs_s = _decay_tables(dec_seq, dec_pad, dk, dv)

    def pad_steps(r):
        t = from_rows(r, dec_seq)
        return jnp.pad(t, ((0, 0), (0, dec_pad - dec_seq), (0, 0))).reshape(bs * dec_pad, r.shape[-1])

    conv0_p = jnp.zeros((SUBLANES, d_ff), F32)
    pool_p, pool_s, vn_s, conv_p, conv_s = [], [], [], [], []
    ret_p = ret_s = None
    h = _norm(*x, norm_mix_pre[0])
    for l in range(depth):
        if l % 2 == 0:
            e = l // 2
            pw = dict(layer=e, tn=a_width, n_out=a_width)
            a, w_o = _proj(h, w_mix_in, col0=0, out_dtype=F32, epilogue="none", name="proj_a", side=(w_mix_out, e),
                           **pw)
            u = _proj(h, w_mix_in, col0=1, out_dtype=BF16, epilogue="gelu", name="proj_u", **pw)
            vn = _proj(h, w_mix_in, col0=2, out_dtype=F32, epilogue="gelu_ln",
                       extras=(sgu_norm_g[e].reshape(1, b_width), sgu_norm_b[e].reshape(1, b_width)),
                       extra_specs=(pl.BlockSpec((1, b_width), lambda j, i: (0, 0)),) * 2, name="proj_v", **pw)
            ws_p = jnp.where(tril_p, w_spatial[e][:, :SGU_CHUNK, :SGU_CHUNK], 0.0)
            ws_s = jnp.where(tril_s, w_spatial[e][:, :dec_seq, :dec_seq], 0.0)
            wmix_s = jnp.einsum("hij,bc->hibjc", ws_s, eye_b).reshape(SGU_HEADS, m_s, m_s)
            bias_p = jnp.broadcast_to(b_spatial[e][:, :SGU_CHUNK, None], (SGU_HEADS, SGU_CHUNK, hdim))
            bias_s = jnp.broadcast_to(b_spatial[e][:, :dec_seq, None, None],
                                      (SGU_HEADS, dec_seq, bs, hdim)).reshape(SGU_HEADS, m_s, hdim)
            halo_s = state_pool[e].transpose(1, 0, 2).reshape(POOL_BUF * bs, a_width)
            mixed = _mix(a, u, vn, invc, halo_s, w_pool_grp[e].astype(BF16), pool_scale[e].reshape(1, a_width),
                         ws_p.astype(BF16), bias_p, wmix_s.astype(BF16), bias_s, **geom)
            pool_p.append(jnp.stack([a[(b + 1) * seq - POOL_BUF:(b + 1) * seq] for b in range(bp)]))
            pool_s.append(jnp.concatenate([state_pool[e][:, dec_seq:], from_rows(a[m_p:], dec_seq)], axis=1))
            vn_s.append(from_rows(vn[m_p:], dec_seq))
        else:
            r = l // 2
            pw = dict(layer=r, col0=0, tn=1024, out_dtype=BF16)
            q = _proj(h, w_q, n_out=w_q.shape[2], epilogue="rotary", extras=rot, extra_specs=rot_specs,
                      name="proj_q", **pw)
            k = _proj(h, w_k, n_out=w_k.shape[2], epilogue="rotary", extras=rot, extra_specs=rot_specs,
                      scale=dk ** -0.5, name="proj_k", **pw)
            v = _proj(h, w_v, n_out=w_v.shape[2], epilogue="none", name="proj_v_ret", **pw)
            g, w_o = _proj(h, w_g, n_out=w_g.shape[2], epilogue="silu", name="proj_g", side=(w_ret_out, r), **pw)
            gn = ret_norm_g[r].reshape(RET_HEADS, 1, dv)
            qkvg = (q, k, v, g)
            gated, gated_s, ret_p, ret_s = _retention(
                qkvg, pad_steps(jnp.concatenate([t[m_p:] for t in qkvg], axis=1)), state_ret, ret_p, ret_s, tabs_p,
                tabs_s, gn, layer=r,
                n_layers=n_ret, n_seq_p=bp, n_chunk=seq // RET_CHUNK, chunk=RET_CHUNK, n_seq_s=bs, rows_s=dec_pad)
            gated_s = to_rows(gated_s.reshape(bs, dec_pad, -1)[:, :dec_seq])
            mixed = (gated, gated_s)
        x, h = _out_proj(mixed, w_o, x, norm_mix_post[l], norm_ffn_pre[l], tm=TM_OUT, m_first=m_p, name="mix_out")
        wc, bc = w_dconv, b_dconv.reshape(depth, 1, d_ff)
        act_p, tail_p, wgb, wub, wdb = _ffn_in_prompt(h, w_ffn_gate, w_ffn_up, wc, bc, conv0_p, w_ffn_down, layer=l,
                                                      tm=TM_FFN, tn=TN_FFN, n_tiles=m_p // TM_FFN,
                                                      tiles_per_seq=seq // TM_FFN)
        act_s, tail_s = _ffn_in_sample(h, wgb, wub, wc, bc, to_rows(state_conv[l]), layer=l, tm=m_s, tn=TN_FFN,
                                       row0=m_p // m_s, shift=bs)
        conv_p.append(tail_p.reshape(bp, seq // TM_FFN, SUBLANES, d_ff)[:, -1, SUBLANES - (CONV_W - 1):])
        conv_s.append(from_rows(tail_s, CONV_W - 1))
        g_next = norm_mix_pre[l + 1] if l + 1 < depth else None
        x, h = _out_proj((act_p, act_s), wdb, x, norm_ffn_post[l], g_next, tm=TM_DOWN, m_first=m_p, name="ffn_out")

    y_prompt = x[0].reshape(bp, seq, d)
    y_sample = from_rows(x[1], dec_seq)
    return (y_prompt, y_sample, jnp.stack(pool_p), jnp.stack(pool_s), jnp.stack(vn_s),
            ret_p, ret_s, jnp.stack(conv_p), jnp.stack(conv_s))
```

```python
import functools

import jax
import jax.numpy as jnp
import numpy as np
from jax import lax
from jax.experimental import pallas as pl
from jax.experimental.pallas import tpu as pltpu

F32 = jnp.float32
BF16 = jnp.bfloat16

PAST_LEN = 16384
POOL_WINDOWS = (2, 4, 8, 16)
POOL_BUF = max(POOL_WINDOWS) - 1
SGU_HEADS = 4
SGU_CHUNK = 128
RET_HEADS = 8
RET_CHUNK = 128
ROPE_BASE = 10000.0
CONV_W = 3
EPS = 1e-6

V7X_VMEM_BYTES = 64 * 1024 * 1024
VMEM_LIMIT_BYTES = V7X_VMEM_BYTES - 4 * 1024 * 1024
SUBLANES = 8
LANES = 128
BF16_SUBLANES = 16

TM = 512
TM_PROJ = 1088
TM_FFN = 1024
TN_FFN = 512
TM_OUT = 256
TM_DOWN = 256


def _cparams(n_axes):
    return pltpu.CompilerParams(
        dimension_semantics=("arbitrary",) * n_axes, vmem_limit_bytes=VMEM_LIMIT_BYTES)


def _resident(shape):
    zeros = (0,) * len(shape)
    return pl.BlockSpec(shape, lambda *_: zeros, pipeline_mode=pl.Buffered(1))


def _rms(x, g):
    return x * lax.rsqrt(jnp.mean(x * x, axis=-1, keepdims=True) + EPS) * g


def _layer_norm(x, g):
    mu = jnp.mean(x, axis=-1, keepdims=True)
    xc = x - mu
    return xc * lax.rsqrt(jnp.mean(xc * xc, axis=-1, keepdims=True) + EPS) * g


def _first_spec(tm, d, n_first):
    return pl.BlockSpec((tm, d), lambda i: (jnp.minimum(i, n_first - 1), 0))


def _second_spec(tm, d, n_first):
    return pl.BlockSpec((tm, d), lambda i: (jnp.maximum(i - n_first, 0), 0))


def _norm_kernel(xp_ref, xs_ref, g_ref, h_ref, *, n_first):
    i = pl.program_id(0)

    @pl.when(i < n_first)
    def _():
        h_ref[...] = _rms(xp_ref[...], g_ref[...]).astype(BF16)

    @pl.when(i >= n_first)
    def _():
        h_ref[...] = _rms(xs_ref[...], g_ref[...]).astype(BF16)


def _norm(xp, xs, g):
    d = xp.shape[1]
    m = xp.shape[0] + xs.shape[0]
    n_first = xp.shape[0] // TM
    return pl.pallas_call(
        functools.partial(_norm_kernel, n_first=n_first),
        grid=(m // TM,),
        in_specs=[_first_spec(TM, d, n_first), _second_spec(TM, d, n_first), _resident((1, d))],
        out_specs=pl.BlockSpec((TM, d), lambda i: (i, 0)),
        out_shape=jax.ShapeDtypeStruct((m, d), BF16),
        compiler_params=_cparams(1),
        name="norm",
    )(xp, xs, g.reshape(1, d))


def _proj_kernel(h_ref, w_ref, *rest, epilogue, scale, side_cast):
    if side_cast:
        *extras, side_ref, o_ref, side_out_ref, wb_ref = rest
        side_out_ref[...] = side_ref[...].astype(BF16)
    else:
        *extras, o_ref, wb_ref = rest

    @pl.when(pl.program_id(1) == 0)
    def _():
        wb_ref[...] = w_ref[...].astype(BF16)

    z = jnp.dot(h_ref[...], wb_ref[...], preferred_element_type=F32)
    if epilogue == "none":
        o_ref[...] = z.astype(o_ref.dtype)
    elif epilogue == "gelu":
        o_ref[...] = jax.nn.gelu(z, approximate=True).astype(o_ref.dtype)
    elif epilogue == "silu":
        o_ref[...] = jax.nn.silu(z).astype(o_ref.dtype)
    elif epilogue == "gelu_ln":
        g_ref, b_ref = extras
        v = jax.nn.gelu(z, approximate=True)
        o_ref[...] = (_layer_norm(v, g_ref[...]) + b_ref[...]).astype(o_ref.dtype)
    elif epilogue == "rotary":
        cos_ref, sin_ref = extras
        c = cos_ref[...]
        s = sin_ref[...]
        half = c.shape[-1]
        for hd in range(z.shape[-1] // (2 * half)):
            lo = hd * 2 * half
            x1 = z[:, lo:lo + half]
            x2 = z[:, lo + half:lo + 2 * half]
            o_ref[:, lo:lo + half] = ((x1 * c - x2 * s) * scale).astype(o_ref.dtype)
            o_ref[:, lo + half:lo + 2 * half] = ((x1 * s + x2 * c) * scale).astype(o_ref.dtype)
    else:
        raise ValueError(epilogue)


def _proj(h, w, *, layer, col0, n_out, tn, out_dtype, epilogue, extras=(), extra_specs=(), scale=1.0, name,
          side=None):
    m, k = h.shape
    assert m % TM_PROJ == 0 and n_out % tn == 0
    n_i = m // TM_PROJ
    in_specs = [pl.BlockSpec((TM_PROJ, k), lambda j, i: (i, 0)),
                pl.BlockSpec((None, k, tn), lambda j, i: (layer, 0, col0 + j)), *extra_specs]
    args = [h, w, *extras]
    out_specs = [pl.BlockSpec((TM_PROJ, tn), lambda j, i: (i, j))]
    out_shape = [jax.ShapeDtypeStruct((m, n_out), out_dtype)]
    if side is not None:
        w2, layer2 = side
        rows = w2.shape[1] // ((n_out // tn) * n_i)
        assert rows * (n_out // tn) * n_i == w2.shape[1] and rows % BF16_SUBLANES == 0
        in_specs.append(pl.BlockSpec((None, rows, w2.shape[2]), lambda j, i: (layer2, j * n_i + i, 0)))
        args.append(w2)
        out_specs.append(pl.BlockSpec((rows, w2.shape[2]), lambda j, i: (j * n_i + i, 0)))
        out_shape.append(jax.ShapeDtypeStruct(w2.shape[1:], BF16))
    res = pl.pallas_call(
        functools.partial(_proj_kernel, epilogue=epilogue, scale=scale, side_cast=side is not None),
        grid=(n_out // tn, n_i),
        in_specs=in_specs,
        out_specs=out_specs,
        out_shape=out_shape,
        scratch_shapes=[pltpu.VMEM((k, tn), BF16)],
        compiler_params=_cparams(2),
        name=name,
    )(*args)
    return res[0] if side is None else tuple(res)


def _out_kernel(*refs, n_first, split_lhs, split_x, last):
    refs = list(refs)
    lhs_refs = [refs.pop(0) for _ in range(2 if split_lhs else 1)]
    wb_ref = refs.pop(0)
    x_refs = [refs.pop(0) for _ in range(2 if split_x else 1)]
    gpost_ref, gnext_ref, *outs = refs
    i = pl.program_id(0)

    def pick(pair):
        return jnp.where(i < n_first, pair[0][...], pair[1][...]) if len(pair) == 2 else pair[0][...]

    y = jnp.dot(pick(lhs_refs), wb_ref[...], preferred_element_type=F32)
    x_new = pick(x_refs) + _rms(y, gpost_ref[...])
    if last:
        yp_ref, ys_ref = outs

        @pl.when(i < n_first)
        def _():
            yp_ref[...] = x_new

        @pl.when(i >= n_first)
        def _():
            ys_ref[...] = x_new
    else:
        xo_ref, h_ref = outs
        xo_ref[...] = x_new
        h_ref[...] = _rms(x_new, gnext_ref[...]).astype(BF16)


def _out_proj(lhs, w, x, g_post, g_next, *, tm, m_first, name):
    k, d = w.shape
    n_first = m_first // tm
    split_lhs, split_x = isinstance(lhs, tuple), isinstance(x, tuple)
    m = sum(t.shape[0] for t in lhs) if split_lhs else lhs.shape[0]
    last = g_next is None

    def pair(width):
        return [_first_spec(tm, width, n_first), _second_spec(tm, width, n_first)]

    def operand(t, width):
        return (list(t), pair(width)) if isinstance(t, tuple) else ([t], [pl.BlockSpec((tm, width), lambda i: (i, 0))])

    lhs_args, lhs_specs = operand(lhs, k)
    x_args, x_specs = operand(x, d)
    if last:
        g_next = g_post
        out_specs = pair(d)
        out_shape = [jax.ShapeDtypeStruct((m_first, d), F32), jax.ShapeDtypeStruct((m - m_first, d), F32)]
    else:
        out_specs = [pl.BlockSpec((tm, d), lambda i: (i, 0))] * 2
        out_shape = [jax.ShapeDtypeStruct((m, d), F32), jax.ShapeDtypeStruct((m, d), BF16)]
    res = pl.pallas_call(
        functools.partial(_out_kernel, n_first=n_first, split_lhs=split_lhs, split_x=split_x, last=last),
        grid=(m // tm,),
        in_specs=[*lhs_specs, _resident((k, d)), *x_specs, _resident((1, d)), _resident((1, d))],
        out_specs=out_specs,
        out_shape=out_shape,
        compiler_params=_cparams(1),
        name=name,
    )(*lhs_args, w, *x_args, g_post.reshape(1, d), g_next.reshape(1, d))
    return ((res[0], res[1]), None) if last else (res[0], res[1])


def _mix_group(a_ref, u_ref, vn_ref, wgrp_ref, pscale_ref, wmix_ref, bias_ref, o_ref, ext_ref,
               inv_cnt, halo_rows, shift, chunk):
    tm, a_width = a_ref.shape
    gdim = a_width // len(POOL_WINDOWS)
    ext_ref[halo_rows:halo_rows + tm, :] = a_ref[...]
    for gi, w in enumerate(POOL_WINDOWS):
        c0, c1 = gi * gdim, (gi + 1) * gdim
        s = ext_ref[halo_rows:halo_rows + tm, c0:c1]
        for j in range(1, w):
            s = s + ext_ref[halo_rows - j * shift:halo_rows - j * shift + tm, c0:c1]
        d = (s * inv_cnt(gi) - a_ref[:, c0:c1]).astype(BF16)
        z = jnp.dot(d, wgrp_ref[gi], preferred_element_type=F32)
        o_ref[:, c0:c1] = (z * pscale_ref[:, c0:c1]).astype(o_ref.dtype)
    hdim = vn_ref.shape[1] // SGU_HEADS
    for c in range(tm // chunk):
        r0, r1 = c * chunk, (c + 1) * chunk
        for hd in range(SGU_HEADS):
            c0, c1 = hd * hdim, (hd + 1) * hdim
            mixed = jnp.dot(wmix_ref[hd], vn_ref[r0:r1, c0:c1].astype(BF16),
                            preferred_element_type=F32) + bias_ref[hd]
            o_ref[r0:r1, a_width + c0:a_width + c1] = (
                u_ref[r0:r1, c0:c1].astype(F32) * mixed).astype(o_ref.dtype)


def _mix_kernel(a_ref, u_ref, vn_ref, invc_ref, pstate_ref, wgrp_ref, pscale_ref,
                wmix_p_ref, bias_p_ref, wmix_s_ref, bias_s_ref, o_ref, pnew_ref, ext_ref,
                *, n_prompt_tiles, tiles_per_seq, dec_batch):
    i = pl.program_id(0)
    tm = a_ref.shape[0]
    gl = LANES
    halo_p = 2 * SUBLANES

    @pl.when(i < n_prompt_tiles)
    def _prompt():
        @pl.when(i % tiles_per_seq == 0)
        def _():
            ext_ref[0:halo_p, :] = jnp.zeros((halo_p, ext_ref.shape[1]), F32)

        def inv_cnt(gi):
            blk = invc_ref[:, gi * gl:(gi + 1) * gl]
            return jnp.concatenate([blk, blk], axis=1)

        _mix_group(a_ref, u_ref, vn_ref, wgrp_ref, pscale_ref, wmix_p_ref, bias_p_ref, o_ref, ext_ref,
                   inv_cnt, halo_p, 1, SGU_CHUNK)
        ext_ref[0:halo_p, :] = ext_ref[tm:tm + halo_p, :]

    @pl.when(i >= n_prompt_tiles)
    def _sample():
        halo_s = POOL_BUF * dec_batch
        for t in range(POOL_BUF):
            ext_ref[t * dec_batch:(t + 1) * dec_batch, :] = pstate_ref[:, t, :]
        _mix_group(a_ref, u_ref, vn_ref, wgrp_ref, pscale_ref, wmix_s_ref, bias_s_ref, o_ref, ext_ref,
                   lambda gi: 1.0 / POOL_WINDOWS[gi], halo_s, dec_batch, tm)
        for t in range(POOL_BUF):
            src = tm + t * dec_batch
            pnew_ref[:, t, :] = ext_ref[src:src + dec_batch, :]


def _mix(a, u, vn, invc, pstate_all, wgrp, pscale, wmix_p, bias_p, wmix_s, bias_s, *, layer, n_prompt_tiles,
         tiles_per_seq, dec_batch):
    m, a_width = a.shape
    pshape = pstate_all.shape[1:]
    b_width = u.shape[1]
    row = lambda i: (i, 0)
    ext_rows = max(2 * SUBLANES, POOL_BUF * dec_batch) + TM
    return pl.pallas_call(
        functools.partial(_mix_kernel, n_prompt_tiles=n_prompt_tiles, tiles_per_seq=tiles_per_seq,
                          dec_batch=dec_batch),
        grid=(m // TM,),
        in_specs=[pl.BlockSpec((TM, a_width), row), pl.BlockSpec((TM, b_width), row),
                  pl.BlockSpec((TM, b_width), row),
                  pl.BlockSpec((TM, invc.shape[1]), lambda i: (jnp.minimum(i, n_prompt_tiles - 1) % tiles_per_seq, 0)),
                  pl.BlockSpec((None,) + pshape, lambda i: (layer, 0, 0, 0), pipeline_mode=pl.Buffered(1)),
                  _resident(wgrp.shape), _resident(pscale.shape),
                  _resident(wmix_p.shape), _resident(bias_p.shape), _resident(wmix_s.shape),
                  _resident(bias_s.shape)],
        out_specs=[pl.BlockSpec((TM, a_width + b_width), row), pl.BlockSpec(pshape, lambda i: (0, 0, 0))],
        out_shape=[jax.ShapeDtypeStruct((m, a_width + b_width), BF16), jax.ShapeDtypeStruct(pshape, F32)],
        scratch_shapes=[pltpu.VMEM((ext_rows, a_width), F32)],
        compiler_params=_cparams(1),
        name="mix",
    )(a, u, vn, invc, pstate_all, wgrp, pscale, wmix_p, bias_p, wmix_s, bias_s)


def _conv_gelu_gate(gate, prev2, prev1, up, wc_ref, bc_ref):
    conv = bc_ref[...] + prev2 * wc_ref[0:1, :]
    conv = conv + prev1 * wc_ref[1:2, :]
    conv = conv + gate * wc_ref[2:3, :]
    return (jax.nn.gelu(conv, approximate=True) * up).astype(BF16)


def _ffn_in_prompt_kernel(h_ref, wg_ref, wu_ref, wc_ref, bc_ref, cstate_ref, wd_ref, act_ref, tail_ref, wgb_ref,
                          wub_ref, wdb_ref, halo_ref, *, tiles_per_seq):
    i = pl.program_id(1)
    wdb_ref[...] = wd_ref[...].astype(BF16)
    tm = h_ref.shape[0]
    halo = cstate_ref.shape[0]

    @pl.when(i == 0)
    def _():
        wgb_ref[...] = wg_ref[...].astype(BF16)
        wub_ref[...] = wu_ref[...].astype(BF16)
        halo_ref[...] = cstate_ref[...]

    h = h_ref[...]
    gate = jnp.dot(h, wgb_ref[...], preferred_element_type=F32)
    up = jnp.dot(h, wub_ref[...], preferred_element_type=F32)
    before = jnp.where((i % tiles_per_seq) == 0, cstate_ref[...], halo_ref[...])
    row = lax.broadcasted_iota(jnp.int32, before.shape, 0)

    def shifted(s):
        r = pltpu.roll(gate, s, axis=0)
        first = jnp.where(row < s, pltpu.roll(before, s, axis=0), r[0:halo])
        return jnp.concatenate([first, r[halo:]], axis=0)

    act_ref[...] = _conv_gelu_gate(gate, shifted(2), shifted(1), up, wc_ref, bc_ref)
    tail = gate[tm - halo:tm, :]
    tail_ref[...] = tail
    halo_ref[...] = tail


def _ffn_in_sample_kernel(h_ref, wgb_ref, wub_ref, wc_ref, bc_ref, cstate_ref, act_ref, cnew_ref, ext_ref):
    tm = h_ref.shape[0]
    shift, n_state = cstate_ref.shape[0], cstate_ref.shape[1]
    halo = n_state * shift
    h = h_ref[...]
    gate = jnp.dot(h, wgb_ref[...], preferred_element_type=F32)
    up = jnp.dot(h, wub_ref[...], preferred_element_type=F32)
    for t in range(n_state):
        ext_ref[t * shift:(t + 1) * shift, :] = cstate_ref[:, t, :]
    ext_ref[halo:halo + tm, :] = gate
    act_ref[...] = _conv_gelu_gate(gate, ext_ref[halo - 2 * shift:halo - 2 * shift + tm, :],
                                   ext_ref[halo - shift:halo - shift + tm, :], up, wc_ref, bc_ref)
    for t in range(n_state):
        cnew_ref[:, t, :] = gate[tm - halo + t * shift:tm - halo + (t + 1) * shift, :]


def _ffn_in_prompt(h, wg, wu, wc, bc, cstate, wd, *, layer, tm, tn, n_tiles, tiles_per_seq):
    k = h.shape[1]
    n = wg.shape[2]
    halo = cstate.shape[0]
    steps = (n // tn) * n_tiles
    wd_rows = wd.shape[1] // steps
    assert wd_rows * steps == wd.shape[1] and wd_rows % BF16_SUBLANES == 0
    d_out = wd.shape[2]
    wspec = pl.BlockSpec((None, k, tn), lambda j, i: (layer, 0, j))
    wbspec = pl.BlockSpec((k, tn), lambda j, i: (0, j))
    return pl.pallas_call(
        functools.partial(_ffn_in_prompt_kernel, tiles_per_seq=tiles_per_seq),
        grid=(n // tn, n_tiles),
        in_specs=[pl.BlockSpec((tm, k), lambda j, i: (i, 0)), wspec, wspec,
                  pl.BlockSpec((None, CONV_W, tn), lambda j, i: (layer, 0, j)),
                  pl.BlockSpec((None, 1, tn), lambda j, i: (layer, 0, j)),
                  pl.BlockSpec((halo, tn), lambda j, i: (0, j)),
                  pl.BlockSpec((None, wd_rows, d_out), lambda j, i: (layer, j * n_tiles + i, 0))],
        out_specs=[pl.BlockSpec((tm, tn), lambda j, i: (i, j)), pl.BlockSpec((halo, tn), lambda j, i: (i, j)),
                   wbspec, wbspec, pl.BlockSpec((wd_rows, d_out), lambda j, i: (j * n_tiles + i, 0))],
        out_shape=[jax.ShapeDtypeStruct((n_tiles * tm, n), BF16), jax.ShapeDtypeStruct((n_tiles * halo, n), F32),
                   jax.ShapeDtypeStruct((k, n), BF16), jax.ShapeDtypeStruct((k, n), BF16),
                   jax.ShapeDtypeStruct(wd.shape[1:], BF16)],
        scratch_shapes=[pltpu.VMEM((halo, tn), F32)],
        compiler_params=_cparams(2),
        name="ffn_in_prompt",
    )(h, wg, wu, wc, bc, cstate, wd)


def _ffn_in_sample(h, wgb, wub, wc, bc, cstate_all, *, layer, tm, tn, row0):
    k = h.shape[1]
    n = wgb.shape[1]
    bs, n_state = cstate_all.shape[1:3]
    wbspec = pl.BlockSpec((k, tn), lambda j: (0, j))
    return pl.pallas_call(
        _ffn_in_sample_kernel,
        grid=(n // tn,),
        in_specs=[pl.BlockSpec((tm, k), lambda j: (row0, 0)), wbspec, wbspec,
                  pl.BlockSpec((None, CONV_W, tn), lambda j: (layer, 0, j)),
                  pl.BlockSpec((None, 1, tn), lambda j: (layer, 0, j)),
                  pl.BlockSpec((None, bs, n_state, tn), lambda j: (layer, 0, 0, j))],
        out_specs=[pl.BlockSpec((tm, tn), lambda j: (0, j)), pl.BlockSpec((bs, n_state, tn), lambda j: (0, 0, j))],
        out_shape=[jax.ShapeDtypeStruct((tm, n), BF16), jax.ShapeDtypeStruct((bs, n_state, n), F32)],
        scratch_shapes=[pltpu.VMEM((n_state * bs + tm, tn), F32)],
        compiler_params=_cparams(1),
        name="ffn_in_sample",
    )(h, wgb, wub, wc, bc, cstate_all)


def _ret_head(hd, q_ref, k_ref, v_ref, g_ref, s_in_ref, s_out_ref, dmask_ref, xi_ref, zeta_ref, gc_ref, gn_ref):
    dk = q_ref.shape[1] // RET_HEADS
    dv = v_ref.shape[1] // RET_HEADS
    qh = q_ref[:, hd * dk:(hd + 1) * dk]
    kh = k_ref[:, hd * dk:(hd + 1) * dk]
    vh = v_ref[:, hd * dv:(hd + 1) * dv]
    state = s_in_ref[0, hd]
    sc = lax.dot_general(qh, kh, (((1,), (1,)), ((), ())), preferred_element_type=F32) * dmask_ref[hd]
    o = jnp.dot(sc.astype(BF16), vh, preferred_element_type=F32)
    o = o + jnp.dot(qh, state.astype(BF16), preferred_element_type=F32) * xi_ref[hd]
    kz = (kh.astype(F32) * zeta_ref[hd]).astype(BF16)
    s_out_ref[0, hd] = gc_ref[hd] * state + lax.dot_general(
        kz, vh, (((0,), (0,)), ((), ())), preferred_element_type=F32)
    on = _layer_norm(o, gn_ref[hd])
    return (g_ref[:, hd * dv:(hd + 1) * dv].astype(F32) * on).astype(BF16)


def _ret_kernel(qp_ref, kp_ref, vp_ref, gp_ref, qs_ref, ks_ref, vs_ref, gs_ref, s0s_ref,
                dmask_p_ref, xi_p_ref, zeta_p_ref, gc_p_ref, dmask_s_ref, xi_s_ref, zeta_s_ref, gc_s_ref, gn_ref,
                *rest):
    op_ref, os_ref, sp_ref, ss_ref = rest[-4:]

    @pl.when(pl.program_id(1) == 0)
    def _():
        sp_ref[...] = jnp.zeros(sp_ref.shape, F32)

    dv = vp_ref.shape[1] // RET_HEADS
    n_s = s0s_ref.shape[0]
    rows_s = qs_ref.shape[0] // n_s
    for hd in range(RET_HEADS):
        cols = slice(hd * dv, (hd + 1) * dv)
        op_ref[:, cols] = _ret_head(hd, qp_ref, kp_ref, vp_ref, gp_ref, sp_ref, sp_ref,
                                    dmask_p_ref, xi_p_ref, zeta_p_ref, gc_p_ref, gn_ref)
        for j in range(n_s):
            rows = pl.ds(j * rows_s, rows_s)
            os_ref[j * rows_s:(j + 1) * rows_s, cols] = _ret_head(
                hd, qs_ref.at[rows], ks_ref.at[rows], vs_ref.at[rows], gs_ref.at[rows], s0s_ref.at[j:j + 1],
                ss_ref.at[j:j + 1], dmask_s_ref, xi_s_ref, zeta_s_ref, gc_s_ref, gn_ref)


def _retention(qkvg_p, qkvg_s, s0_s_all, sp_prev, ss_prev, tabs_p, tabs_s, gn, *, layer, n_layers, n_seq_p, n_chunk,
               chunk, n_seq_s, rows_s):
    steps = n_seq_p * n_chunk
    per_step = n_seq_s // steps
    assert per_step * steps == n_seq_s
    dq, dvv = qkvg_p[0].shape[1], qkvg_p[2].shape[1]
    state_shape = (RET_HEADS, dq // RET_HEADS, dvv // RET_HEADS)
    blk_p = lambda b, c: (b * n_chunk + c, 0)
    blk_s = lambda b, c: (b * n_chunk + c, 0)
    sp_spec = pl.BlockSpec((None, 1) + state_shape, lambda b, c: (layer, b, 0, 0, 0))
    ss_spec = pl.BlockSpec((None, per_step) + state_shape, lambda b, c: (layer, b * n_chunk + c, 0, 0, 0))
    in_specs = [pl.BlockSpec((chunk, t.shape[1]), blk_p) for t in qkvg_p]
    in_specs += [pl.BlockSpec((per_step * rows_s, t.shape[1]), blk_s) for t in qkvg_s]
    in_specs.append(ss_spec)
    args = [*qkvg_p, *qkvg_s, s0_s_all]
    for t in (*tabs_p, *tabs_s, gn):
        in_specs.append(_resident(t.shape))
        args.append(t)
    aliases = {}
    for prev, out_idx in ((sp_prev, 2), (ss_prev, 3)):
        if prev is not None:
            in_specs.append(pl.BlockSpec(memory_space=pl.ANY))
            args.append(prev)
            aliases[len(args) - 1] = out_idx
    return pl.pallas_call(
        _ret_kernel,
        grid=(n_seq_p, n_chunk),
        in_specs=in_specs,
        out_specs=[pl.BlockSpec((chunk, dvv), blk_p), pl.BlockSpec((per_step * rows_s, dvv), blk_s), sp_spec, ss_spec],
        out_shape=[jax.ShapeDtypeStruct((steps * chunk, dvv), BF16),
                   jax.ShapeDtypeStruct((n_seq_s * rows_s, dvv), BF16),
                   jax.ShapeDtypeStruct((n_layers, n_seq_p) + state_shape, F32),
                   jax.ShapeDtypeStruct((n_layers, n_seq_s) + state_shape, F32)],
        input_output_aliases=aliases,
        compiler_params=_cparams(2),
        name="retention",
    )(*args)


def _decay_tables(c_true, c_pad, dk, dv):
    f32 = np.float32
    lg = np.log1p(-np.exp2(f32(-5.0) - np.arange(RET_HEADS, dtype=f32)))
    idx = np.arange(c_true, dtype=f32)
    diff = idx[:, None] - idx[None, :]
    dmask = np.where(diff >= 0, np.exp(lg[:, None, None] * np.maximum(diff, f32(0.0))), f32(0.0))
    xi = np.exp(lg[:, None] * (idx + f32(1.0)))
    zeta = np.exp(lg[:, None] * (f32(c_true - 1.0) - idx))
    g_c = np.exp(lg * f32(c_true))
    pad = c_pad - c_true
    dmask = np.pad(dmask, ((0, 0), (0, pad), (0, pad)))
    xi = np.pad(xi, ((0, 0), (0, pad)))
    zeta = np.pad(zeta, ((0, 0), (0, pad)))
    tabs = (dmask,
            np.broadcast_to(xi[:, :, None], (RET_HEADS, c_pad, dv)),
            np.broadcast_to(zeta[:, :, None], (RET_HEADS, c_pad, dk)),
            np.broadcast_to(g_c[:, None, None], (RET_HEADS, 1, dv)))
    return tuple(jnp.asarray(t, dtype=F32) for t in tabs)


def kernel(x_prompt, x_sample, state_pool, state_ret, state_conv, w_mix_in, w_pool_grp, pool_scale, w_spatial,
           b_spatial, sgu_norm_g, sgu_norm_b, w_mix_out, w_q, w_k, w_v, w_g, ret_norm_g, w_ret_out, norm_mix_pre,
           norm_mix_post, norm_ffn_pre, norm_ffn_post, w_ffn_gate, w_ffn_up, w_dconv, b_dconv, w_ffn_down):
    bp, seq, d = x_prompt.shape
    bs, dec_seq, _ = x_sample.shape
    depth = norm_mix_pre.shape[0]
    n_ret = w_q.shape[0]
    a_width = w_pool_grp.shape[1] * w_pool_grp.shape[2]
    b_width = sgu_norm_g.shape[1]
    dk = w_q.shape[2] // RET_HEADS
    dv = w_v.shape[2] // RET_HEADS
    d_ff = w_ffn_gate.shape[2]
    m_p, m_s = bp * seq, bs * dec_seq
    m = m_p + m_s
    assert m_s == TM and seq % TM_FFN == 0 and seq % RET_CHUNK == 0 and seq >= POOL_BUF and a_width == b_width
    assert CONV_W - 1 <= dec_seq < POOL_BUF and dec_seq <= SGU_CHUNK and dec_seq <= BF16_SUBLANES
    n_prompt_tiles = m_p // TM
    tiles_per_seq = seq // TM
    geom = dict(n_prompt_tiles=n_prompt_tiles, tiles_per_seq=tiles_per_seq, dec_batch=bs)

    def to_rows(t):
        return t.transpose(1, 0, 2).reshape(t.shape[1] * bs, t.shape[-1])

    def from_rows(r, steps):
        return r.reshape(steps, bs, r.shape[-1]).transpose(1, 0, 2)

    x = (x_prompt.reshape(m_p, d), to_rows(x_sample))

    half = dk // 2
    inv = np.float32(ROPE_BASE) ** (-np.arange(half, dtype=np.float32) / np.float32(half))
    pos_p = np.arange(seq).astype(np.float32)
    pos_s = (PAST_LEN + np.arange(dec_seq)).astype(np.float32)
    ang = np.concatenate([np.tile(pos_p[:, None] * inv[None, :], (bp, 1)),
                          np.repeat(pos_s[:, None] * inv[None, :], bs, axis=0)], axis=0)
    rot = (jnp.asarray(np.cos(ang), dtype=F32), jnp.asarray(np.sin(ang), dtype=F32))
    rot_specs = (pl.BlockSpec((TM_PROJ, half), lambda j, i: (i, 0)),) * 2

    pos = np.arange(seq)
    invc = jnp.asarray(np.concatenate(
        [np.broadcast_to((np.float32(1.0) / np.minimum(pos + 1, w).astype(np.float32))[:, None], (seq, LANES))
         for w in POOL_WINDOWS], axis=1), dtype=F32)

    tril_p = np.tril(np.ones((SGU_CHUNK, SGU_CHUNK), dtype=bool))
    tril_s = np.tril(np.ones((dec_seq, dec_seq), dtype=bool))
    eye_b = np.eye(bs, dtype=np.float32)
    hdim = b_width // SGU_HEADS

    dec_pad = BF16_SUBLANES
    tabs_p = _decay_tables(RET_CHUNK, RET_CHUNK, dk, dv)
    tabs_s = _decay_tables(dec_seq, dec_pad, dk, dv)

    def pad_steps(r):
        t = from_rows(r, dec_seq)
        return jnp.pad(t, ((0, 0), (0, dec_pad - dec_seq), (0, 0))).reshape(bs * dec_pad, r.shape[-1])

    conv0_p = jnp.zeros((SUBLANES, d_ff), F32)
    pool_p, pool_s, vn_s, conv_p, conv_s = [], [], [], [], []
    ret_p = ret_s = None
    h = _norm(*x, norm_mix_pre[0])
    for l in range(depth):
        if l % 2 == 0:
            e = l // 2
            pw = dict(layer=e, tn=a_width, n_out=a_width)
            a, w_o = _proj(h, w_mix_in, col0=0, out_dtype=F32, epilogue="none", name="proj_a", side=(w_mix_out, e),
                           **pw)
            u = _proj(h, w_mix_in, col0=1, out_dtype=BF16, epilogue="gelu", name="proj_u", **pw)
            vn = _proj(h, w_mix_in, col0=2, out_dtype=F32, epilogue="gelu_ln",
                       extras=(sgu_norm_g[e].reshape(1, b_width), sgu_norm_b[e].reshape(1, b_width)),
                       extra_specs=(pl.BlockSpec((1, b_width), lambda j, i: (0, 0)),) * 2, name="proj_v", **pw)
            ws_p = jnp.where(tril_p, w_spatial[e][:, :SGU_CHUNK, :SGU_CHUNK], 0.0)
            ws_s = jnp.where(tril_s, w_spatial[e][:, :dec_seq, :dec_seq], 0.0)
            wmix_s = jnp.einsum("hij,bc->hibjc", ws_s, eye_b).reshape(SGU_HEADS, m_s, m_s)
            bias_p = jnp.broadcast_to(b_spatial[e][:, :SGU_CHUNK, None], (SGU_HEADS, SGU_CHUNK, hdim))
            bias_s = jnp.broadcast_to(b_spatial[e][:, :dec_seq, None, None],
                                      (SGU_HEADS, dec_seq, bs, hdim)).reshape(SGU_HEADS, m_s, hdim)
            mixed, pool_new = _mix(a, u, vn, invc, state_pool, w_pool_grp[e].astype(BF16),
                                   pool_scale[e].reshape(1, a_width), ws_p.astype(BF16), bias_p, wmix_s.astype(BF16),
                                   bias_s, layer=e, **geom)
            pool_p.append(jnp.stack([a[(b + 1) * seq - POOL_BUF:(b + 1) * seq] for b in range(bp)]))
            pool_s.append(pool_new)
            vn_s.append(from_rows(vn[m_p:], dec_seq))
        else:
            r = l // 2
            pw = dict(layer=r, col0=0, tn=1024, out_dtype=BF16)
            q = _proj(h, w_q, n_out=w_q.shape[2], epilogue="rotary", extras=rot, extra_specs=rot_specs,
                      name="proj_q", **pw)
            k = _proj(h, w_k, n_out=w_k.shape[2], epilogue="rotary", extras=rot, extra_specs=rot_specs,
                      scale=dk ** -0.5, name="proj_k", **pw)
            v = _proj(h, w_v, n_out=w_v.shape[2], epilogue="none", name="proj_v_ret", **pw)
            g, w_o = _proj(h, w_g, n_out=w_g.shape[2], epilogue="silu", name="proj_g", side=(w_ret_out, r), **pw)
            gn = ret_norm_g[r].reshape(RET_HEADS, 1, dv)
            qkvg = (q, k, v, g)
            gated, gated_s, ret_p, ret_s = _retention(
                qkvg, tuple(pad_steps(t[m_p:]) for t in qkvg), state_ret, ret_p, ret_s, tabs_p, tabs_s, gn, layer=r,
                n_layers=n_ret, n_seq_p=bp, n_chunk=seq // RET_CHUNK, chunk=RET_CHUNK, n_seq_s=bs, rows_s=dec_pad)
            gated_s = to_rows(gated_s.reshape(bs, dec_pad, -1)[:, :dec_seq])
            mixed = (gated, gated_s)
        x, h = _out_proj(mixed, w_o, x, norm_mix_post[l], norm_ffn_pre[l], tm=TM_OUT, m_first=m_p, name="mix_out")
        wc, bc = w_dconv, b_dconv.reshape(depth, 1, d_ff)
        act_p, tail_p, wgb, wub, wdb = _ffn_in_prompt(h, w_ffn_gate, w_ffn_up, wc, bc, conv0_p, w_ffn_down, layer=l,
                                                      tm=TM_FFN, tn=TN_FFN, n_tiles=m_p // TM_FFN,
                                                      tiles_per_seq=seq // TM_FFN)
        act_s, conv_new = _ffn_in_sample(h, wgb, wub, wc, bc, state_conv, layer=l, tm=m_s, tn=TN_FFN,
                                         row0=m_p // m_s)
        conv_p.append(tail_p.reshape(bp, seq // TM_FFN, SUBLANES, d_ff)[:, -1, SUBLANES - (CONV_W - 1):])
        conv_s.append(conv_new)
        g_next = norm_mix_pre[l + 1] if l + 1 < depth else None
        x, h = _out_proj((act_p, act_s), wdb, x, norm_ffn_post[l], g_next, tm=TM_DOWN, m_first=m_p, name="ffn_out")

    y_prompt = x[0].reshape(bp, seq, d)
    y_sample = from_rows(x[1], dec_seq)
    return (y_prompt, y_sample, jnp.stack(pool_p), jnp.stack(pool_s), jnp.stack(vn_s),
            ret_p, ret_s, jnp.stack(conv_p), jnp.stack(conv_s))
```

```python
import functools

import jax
import jax.numpy as jnp
import numpy as np
from jax import lax
from jax.experimental import pallas as pl
from jax.experimental.pallas import tpu as pltpu

F32 = jnp.float32
BF16 = jnp.bfloat16

PAST_LEN = 16384
POOL_WINDOWS = (2, 4, 8, 16)
POOL_BUF = max(POOL_WINDOWS) - 1
SGU_HEADS = 4
SGU_CHUNK = 128
RET_HEADS = 8
RET_CHUNK = 128
ROPE_BASE = 10000.0
CONV_W = 3
EPS = 1e-6

V7X_VMEM_BYTES = 64 * 1024 * 1024
VMEM_LIMIT_BYTES = V7X_VMEM_BYTES - 4 * 1024 * 1024
SUBLANES = 8
LANES = 128
BF16_SUBLANES = 16

TM = 512
TM_PROJ = 1088
TM_FFN = 1024
TN_FFN = 512
TM_OUT = 256
TM_DOWN = 256


def _cparams(n_axes):
    return pltpu.CompilerParams(
        dimension_semantics=("arbitrary",) * n_axes, vmem_limit_bytes=VMEM_LIMIT_BYTES)


def _resident(shape):
    zeros = (0,) * len(shape)
    return pl.BlockSpec(shape, lambda *_: zeros, pipeline_mode=pl.Buffered(1))


def _rms(x, g):
    return x * lax.rsqrt(jnp.mean(x * x, axis=-1, keepdims=True) + EPS) * g


def _layer_norm(x, g):
    mu = jnp.mean(x, axis=-1, keepdims=True)
    xc = x - mu
    return xc * lax.rsqrt(jnp.mean(xc * xc, axis=-1, keepdims=True) + EPS) * g


def _first_spec(tm, d, n_first):
    return pl.BlockSpec((tm, d), lambda i: (jnp.minimum(i, n_first - 1), 0))


def _second_spec(tm, d, n_first):
    return pl.BlockSpec((tm, d), lambda i: (jnp.maximum(i - n_first, 0), 0))


def _norm_kernel(xp_ref, xs_ref, g_ref, h_ref, *, n_first):
    i = pl.program_id(0)

    @pl.when(i < n_first)
    def _():
        h_ref[...] = _rms(xp_ref[...], g_ref[...]).astype(BF16)

    @pl.when(i >= n_first)
    def _():
        h_ref[...] = _rms(xs_ref[...], g_ref[...]).astype(BF16)


def _norm(xp, xs, g):
    d = xp.shape[1]
    m = xp.shape[0] + xs.shape[0]
    n_first = xp.shape[0] // TM
    return pl.pallas_call(
        functools.partial(_norm_kernel, n_first=n_first),
        grid=(m // TM,),
        in_specs=[_first_spec(TM, d, n_first), _second_spec(TM, d, n_first), _resident((1, d))],
        out_specs=pl.BlockSpec((TM, d), lambda i: (i, 0)),
        out_shape=jax.ShapeDtypeStruct((m, d), BF16),
        compiler_params=_cparams(1),
        name="norm",
    )(xp, xs, g.reshape(1, d))


def _proj_kernel(h_ref, w_ref, *rest, epilogue, scale, side_cast):
    if side_cast:
        *extras, side_ref, o_ref, side_out_ref, wb_ref = rest
        side_out_ref[...] = side_ref[...].astype(BF16)
    else:
        *extras, o_ref, wb_ref = rest

    @pl.when(pl.program_id(1) == 0)
    def _():
        wb_ref[...] = w_ref[...].astype(BF16)

    z = jnp.dot(h_ref[...], wb_ref[...], preferred_element_type=F32)
    if epilogue == "none":
        o_ref[...] = z.astype(o_ref.dtype)
    elif epilogue == "gelu":
        o_ref[...] = jax.nn.gelu(z, approximate=True).astype(o_ref.dtype)
    elif epilogue == "silu":
        o_ref[...] = jax.nn.silu(z).astype(o_ref.dtype)
    elif epilogue == "gelu_ln":
        g_ref, b_ref = extras
        v = jax.nn.gelu(z, approximate=True)
        o_ref[...] = (_layer_norm(v, g_ref[...]) + b_ref[...]).astype(o_ref.dtype)
    elif epilogue == "rotary":
        cos_ref, sin_ref = extras
        c = cos_ref[...]
        s = sin_ref[...]
        half = c.shape[-1]
        for hd in range(z.shape[-1] // (2 * half)):
            lo = hd * 2 * half
            x1 = z[:, lo:lo + half]
            x2 = z[:, lo + half:lo + 2 * half]
            o_ref[:, lo:lo + half] = ((x1 * c - x2 * s) * scale).astype(o_ref.dtype)
            o_ref[:, lo + half:lo + 2 * half] = ((x1 * s + x2 * c) * scale).astype(o_ref.dtype)
    else:
        raise ValueError(epilogue)


def _proj(h, w, *, layer, col0, n_out, tn, out_dtype, epilogue, extras=(), extra_specs=(), scale=1.0, name,
          side=None):
    m, k = h.shape
    assert m % TM_PROJ == 0 and n_out % tn == 0
    n_i = m // TM_PROJ
    in_specs = [pl.BlockSpec((TM_PROJ, k), lambda j, i: (i, 0)),
                pl.BlockSpec((None, k, tn), lambda j, i: (layer, 0, col0 + j)), *extra_specs]
    args = [h, w, *extras]
    out_specs = [pl.BlockSpec((TM_PROJ, tn), lambda j, i: (i, j))]
    out_shape = [jax.ShapeDtypeStruct((m, n_out), out_dtype)]
    if side is not None:
        w2, layer2 = side
        rows = w2.shape[1] // ((n_out // tn) * n_i)
        assert rows * (n_out // tn) * n_i == w2.shape[1] and rows % BF16_SUBLANES == 0
        in_specs.append(pl.BlockSpec((None, rows, w2.shape[2]), lambda j, i: (layer2, j * n_i + i, 0)))
        args.append(w2)
        out_specs.append(pl.BlockSpec((rows, w2.shape[2]), lambda j, i: (j * n_i + i, 0)))
        out_shape.append(jax.ShapeDtypeStruct(w2.shape[1:], BF16))
    res = pl.pallas_call(
        functools.partial(_proj_kernel, epilogue=epilogue, scale=scale, side_cast=side is not None),
        grid=(n_out // tn, n_i),
        in_specs=in_specs,
        out_specs=out_specs,
        out_shape=out_shape,
        scratch_shapes=[pltpu.VMEM((k, tn), BF16)],
        compiler_params=_cparams(2),
        name=name,
    )(*args)
    return res[0] if side is None else tuple(res)


def _out_kernel(*refs, n_first, split_lhs, split_x, last):
    refs = list(refs)
    lhs_refs = [refs.pop(0) for _ in range(2 if split_lhs else 1)]
    wb_ref = refs.pop(0)
    x_refs = [refs.pop(0) for _ in range(2 if split_x else 1)]
    gpost_ref, gnext_ref, *outs = refs
    i = pl.program_id(0)

    def pick(pair):
        return jnp.where(i < n_first, pair[0][...], pair[1][...]) if len(pair) == 2 else pair[0][...]

    y = jnp.dot(pick(lhs_refs), wb_ref[...], preferred_element_type=F32)
    x_new = pick(x_refs) + _rms(y, gpost_ref[...])
    if last:
        yp_ref, ys_ref = outs

        @pl.when(i < n_first)
        def _():
            yp_ref[...] = x_new

        @pl.when(i >= n_first)
        def _():
            ys_ref[...] = x_new
    else:
        xo_ref, h_ref = outs
        xo_ref[...] = x_new
        h_ref[...] = _rms(x_new, gnext_ref[...]).astype(BF16)


def _out_proj(lhs, w, x, g_post, g_next, *, tm, m_first, name):
    k, d = w.shape
    n_first = m_first // tm
    split_lhs, split_x = isinstance(lhs, tuple), isinstance(x, tuple)
    m = sum(t.shape[0] for t in lhs) if split_lhs else lhs.shape[0]
    last = g_next is None

    def pair(width):
        return [_first_spec(tm, width, n_first), _second_spec(tm, width, n_first)]

    def operand(t, width):
        return (list(t), pair(width)) if isinstance(t, tuple) else ([t], [pl.BlockSpec((tm, width), lambda i: (i, 0))])

    lhs_args, lhs_specs = operand(lhs, k)
    x_args, x_specs = operand(x, d)
    if last:
        g_next = g_post
        out_specs = pair(d)
        out_shape = [jax.ShapeDtypeStruct((m_first, d), F32), jax.ShapeDtypeStruct((m - m_first, d), F32)]
    else:
        out_specs = [pl.BlockSpec((tm, d), lambda i: (i, 0))] * 2
        out_shape = [jax.ShapeDtypeStruct((m, d), F32), jax.ShapeDtypeStruct((m, d), BF16)]
    res = pl.pallas_call(
        functools.partial(_out_kernel, n_first=n_first, split_lhs=split_lhs, split_x=split_x, last=last),
        grid=(m // tm,),
        in_specs=[*lhs_specs, _resident((k, d)), *x_specs, _resident((1, d)), _resident((1, d))],
        out_specs=out_specs,
        out_shape=out_shape,
        compiler_params=_cparams(1),
        name=name,
    )(*lhs_args, w, *x_args, g_post.reshape(1, d), g_next.reshape(1, d))
    return ((res[0], res[1]), None) if last else (res[0], res[1])


def _mix_group(a_ref, u_ref, vn_ref, wgrp_ref, pscale_ref, wmix_ref, bias_ref, o_ref, ext_ref,
               inv_cnt, halo_rows, shift, chunk):
    tm, a_width = a_ref.shape
    gdim = a_width // len(POOL_WINDOWS)
    ext_ref[halo_rows:halo_rows + tm, :] = a_ref[...]
    if shift == 1:
        assert POOL_WINDOWS == tuple(2 ** (gi + 1) for gi in range(len(POOL_WINDOWS))) and halo_rows > POOL_BUF
        sums, cur = [], ext_ref[0:halo_rows + tm, :]
        for gi in range(len(POOL_WINDOWS)):
            cur = cur + pltpu.roll(cur, 2 ** gi, axis=0)
            sums.append(cur[halo_rows:halo_rows + tm, 0:gdim])
            cur = cur[:, gdim:]
    else:
        sums = []
        for gi, w in enumerate(POOL_WINDOWS):
            s = ext_ref[halo_rows:halo_rows + tm, gi * gdim:(gi + 1) * gdim]
            for j in range(1, w):
                s = s + ext_ref[halo_rows - j * shift:halo_rows - j * shift + tm, gi * gdim:(gi + 1) * gdim]
            sums.append(s)
    for gi, s in enumerate(sums):
        c0, c1 = gi * gdim, (gi + 1) * gdim
        d = (s * inv_cnt(gi) - a_ref[:, c0:c1]).astype(BF16)
        z = jnp.dot(d, wgrp_ref[gi], preferred_element_type=F32)
        o_ref[:, c0:c1] = (z * pscale_ref[:, c0:c1]).astype(o_ref.dtype)
    hdim = vn_ref.shape[1] // SGU_HEADS
    for c in range(tm // chunk):
        r0, r1 = c * chunk, (c + 1) * chunk
        for hd in range(SGU_HEADS):
            c0, c1 = hd * hdim, (hd + 1) * hdim
            mixed = jnp.dot(wmix_ref[hd], vn_ref[r0:r1, c0:c1].astype(BF16),
                            preferred_element_type=F32) + bias_ref[hd]
            o_ref[r0:r1, a_width + c0:a_width + c1] = (
                u_ref[r0:r1, c0:c1].astype(F32) * mixed).astype(o_ref.dtype)


def _mix_kernel(a_ref, u_ref, vn_ref, invc_ref, pstate_ref, wgrp_ref, pscale_ref,
                wmix_p_ref, bias_p_ref, wmix_s_ref, bias_s_ref, o_ref, pnew_ref, ext_ref,
                *, n_prompt_tiles, tiles_per_seq, dec_batch):
    i = pl.program_id(0)
    tm = a_ref.shape[0]
    gl = LANES
    halo_p = 2 * SUBLANES

    @pl.when(i < n_prompt_tiles)
    def _prompt():
        @pl.when(i % tiles_per_seq == 0)
        def _():
            ext_ref[0:halo_p, :] = jnp.zeros((halo_p, ext_ref.shape[1]), F32)

        def inv_cnt(gi):
            blk = invc_ref[:, gi * gl:(gi + 1) * gl]
            return jnp.concatenate([blk, blk], axis=1)

        _mix_group(a_ref, u_ref, vn_ref, wgrp_ref, pscale_ref, wmix_p_ref, bias_p_ref, o_ref, ext_ref,
                   inv_cnt, halo_p, 1, SGU_CHUNK)
        ext_ref[0:halo_p, :] = ext_ref[tm:tm + halo_p, :]

    @pl.when(i >= n_prompt_tiles)
    def _sample():
        halo_s = POOL_BUF * dec_batch
        for t in range(POOL_BUF):
            ext_ref[t * dec_batch:(t + 1) * dec_batch, :] = pstate_ref[:, t, :]
        _mix_group(a_ref, u_ref, vn_ref, wgrp_ref, pscale_ref, wmix_s_ref, bias_s_ref, o_ref, ext_ref,
                   lambda gi: 1.0 / POOL_WINDOWS[gi], halo_s, dec_batch, tm)
        for t in range(POOL_BUF):
            src = tm + t * dec_batch
            pnew_ref[:, t, :] = ext_ref[src:src + dec_batch, :]


def _mix(a, u, vn, invc, pstate_all, wgrp, pscale, wmix_p, bias_p, wmix_s, bias_s, *, layer, n_prompt_tiles,
         tiles_per_seq, dec_batch):
    m, a_width = a.shape
    pshape = pstate_all.shape[1:]
    b_width = u.shape[1]
    row = lambda i: (i, 0)
    ext_rows = max(2 * SUBLANES, POOL_BUF * dec_batch) + TM
    return pl.pallas_call(
        functools.partial(_mix_kernel, n_prompt_tiles=n_prompt_tiles, tiles_per_seq=tiles_per_seq,
                          dec_batch=dec_batch),
        grid=(m // TM,),
        in_specs=[pl.BlockSpec((TM, a_width), row), pl.BlockSpec((TM, b_width), row),
                  pl.BlockSpec((TM, b_width), row),
                  pl.BlockSpec((TM, invc.shape[1]), lambda i: (jnp.minimum(i, n_prompt_tiles - 1) % tiles_per_seq, 0)),
                  pl.BlockSpec((None,) + pshape, lambda i: (layer, 0, 0, 0), pipeline_mode=pl.Buffered(1)),
                  _resident(wgrp.shape), _resident(pscale.shape),
                  _resident(wmix_p.shape), _resident(bias_p.shape), _resident(wmix_s.shape),
                  _resident(bias_s.shape)],
        out_specs=[pl.BlockSpec((TM, a_width + b_width), row), pl.BlockSpec(pshape, lambda i: (0, 0, 0))],
        out_shape=[jax.ShapeDtypeStruct((m, a_width + b_width), BF16), jax.ShapeDtypeStruct(pshape, F32)],
        scratch_shapes=[pltpu.VMEM((ext_rows, a_width), F32)],
        compiler_params=_cparams(1),
        name="mix",
    )(a, u, vn, invc, pstate_all, wgrp, pscale, wmix_p, bias_p, wmix_s, bias_s)


def _conv_gelu_gate(gate, prev2, prev1, up, wc_ref, bc_ref):
    conv = bc_ref[...] + prev2 * wc_ref[0:1, :]
    conv = conv + prev1 * wc_ref[1:2, :]
    conv = conv + gate * wc_ref[2:3, :]
    return (jax.nn.gelu(conv, approximate=True) * up).astype(BF16)


def _ffn_in_prompt_kernel(h_ref, wg_ref, wu_ref, wc_ref, bc_ref, cstate_ref, wd_ref, act_ref, tail_ref, wgb_ref,
                          wub_ref, wdb_ref, halo_ref, *, tiles_per_seq):
    i = pl.program_id(1)
    wdb_ref[...] = wd_ref[...].astype(BF16)
    tm = h_ref.shape[0]
    halo = cstate_ref.shape[0]

    @pl.when(i == 0)
    def _():
        wgb_ref[...] = wg_ref[...].astype(BF16)
        wub_ref[...] = wu_ref[...].astype(BF16)
        halo_ref[...] = cstate_ref[...]

    h = h_ref[...]
    gate = jnp.dot(h, wgb_ref[...], preferred_element_type=F32)
    up = jnp.dot(h, wub_ref[...], preferred_element_type=F32)
    before = jnp.where((i % tiles_per_seq) == 0, cstate_ref[...], halo_ref[...])
    row = lax.broadcasted_iota(jnp.int32, before.shape, 0)

    def shifted(s):
        r = pltpu.roll(gate, s, axis=0)
        first = jnp.where(row < s, pltpu.roll(before, s, axis=0), r[0:halo])
        return jnp.concatenate([first, r[halo:]], axis=0)

    act_ref[...] = _conv_gelu_gate(gate, shifted(2), shifted(1), up, wc_ref, bc_ref)
    tail = gate[tm - halo:tm, :]
    tail_ref[...] = tail
    halo_ref[...] = tail


def _ffn_in_sample_kernel(h_ref, wgb_ref, wub_ref, wc_ref, bc_ref, cstate_ref, act_ref, cnew_ref, ext_ref):
    tm = h_ref.shape[0]
    shift, n_state = cstate_ref.shape[0], cstate_ref.shape[1]
    halo = n_state * shift
    h = h_ref[...]
    gate = jnp.dot(h, wgb_ref[...], preferred_element_type=F32)
    up = jnp.dot(h, wub_ref[...], preferred_element_type=F32)
    for t in range(n_state):
        ext_ref[t * shift:(t + 1) * shift, :] = cstate_ref[:, t, :]
    ext_ref[halo:halo + tm, :] = gate
    act_ref[...] = _conv_gelu_gate(gate, ext_ref[halo - 2 * shift:halo - 2 * shift + tm, :],
                                   ext_ref[halo - shift:halo - shift + tm, :], up, wc_ref, bc_ref)
    for t in range(n_state):
        cnew_ref[:, t, :] = gate[tm - halo + t * shift:tm - halo + (t + 1) * shift, :]


def _ffn_in_prompt(h, wg, wu, wc, bc, cstate, wd, *, layer, tm, tn, n_tiles, tiles_per_seq):
    k = h.shape[1]
    n = wg.shape[2]
    halo = cstate.shape[0]
    steps = (n // tn) * n_tiles
    wd_rows = wd.shape[1] // steps
    assert wd_rows * steps == wd.shape[1] and wd_rows % BF16_SUBLANES == 0
    d_out = wd.shape[2]
    wspec = pl.BlockSpec((None, k, tn), lambda j, i: (layer, 0, j))
    wbspec = pl.BlockSpec((k, tn), lambda j, i: (0, j))
    return pl.pallas_call(
        functools.partial(_ffn_in_prompt_kernel, tiles_per_seq=tiles_per_seq),
        grid=(n // tn, n_tiles),
        in_specs=[pl.BlockSpec((tm, k), lambda j, i: (i, 0)), wspec, wspec,
                  pl.BlockSpec((None, CONV_W, tn), lambda j, i: (layer, 0, j)),
                  pl.BlockSpec((None, 1, tn), lambda j, i: (layer, 0, j)),
                  pl.BlockSpec((halo, tn), lambda j, i: (0, j)),
                  pl.BlockSpec((None, wd_rows, d_out), lambda j, i: (layer, j * n_tiles + i, 0))],
        out_specs=[pl.BlockSpec((tm, tn), lambda j, i: (i, j)), pl.BlockSpec((halo, tn), lambda j, i: (i, j)),
                   wbspec, wbspec, pl.BlockSpec((wd_rows, d_out), lambda j, i: (j * n_tiles + i, 0))],
        out_shape=[jax.ShapeDtypeStruct((n_tiles * tm, n), BF16), jax.ShapeDtypeStruct((n_tiles * halo, n), F32),
                   jax.ShapeDtypeStruct((k, n), BF16), jax.ShapeDtypeStruct((k, n), BF16),
                   jax.ShapeDtypeStruct(wd.shape[1:], BF16)],
        scratch_shapes=[pltpu.VMEM((halo, tn), F32)],
        compiler_params=_cparams(2),
        name="ffn_in_prompt",
    )(h, wg, wu, wc, bc, cstate, wd)


def _ffn_in_sample(h, wgb, wub, wc, bc, cstate_all, *, layer, tm, tn, row0):
    k = h.shape[1]
    n = wgb.shape[1]
    bs, n_state = cstate_all.shape[1:3]
    wbspec = pl.BlockSpec((k, tn), lambda j: (0, j))
    return pl.pallas_call(
        _ffn_in_sample_kernel,
        grid=(n // tn,),
        in_specs=[pl.BlockSpec((tm, k), lambda j: (row0, 0)), wbspec, wbspec,
                  pl.BlockSpec((None, CONV_W, tn), lambda j: (layer, 0, j)),
                  pl.BlockSpec((None, 1, tn), lambda j: (layer, 0, j)),
                  pl.BlockSpec((None, bs, n_state, tn), lambda j: (layer, 0, 0, j))],
        out_specs=[pl.BlockSpec((tm, tn), lambda j: (0, j)), pl.BlockSpec((bs, n_state, tn), lambda j: (0, 0, j))],
        out_shape=[jax.ShapeDtypeStruct((tm, n), BF16), jax.ShapeDtypeStruct((bs, n_state, n), F32)],
        scratch_shapes=[pltpu.VMEM((n_state * bs + tm, tn), F32)],
        compiler_params=_cparams(1),
        name="ffn_in_sample",
    )(h, wgb, wub, wc, bc, cstate_all)


def _ret_head(hd, q_ref, k_ref, v_ref, g_ref, s_in_ref, s_out_ref, dmask_ref, xi_ref, zeta_ref, gc_ref, gn_ref):
    dk = q_ref.shape[1] // RET_HEADS
    dv = v_ref.shape[1] // RET_HEADS
    qh = q_ref[:, hd * dk:(hd + 1) * dk]
    kh = k_ref[:, hd * dk:(hd + 1) * dk]
    vh = v_ref[:, hd * dv:(hd + 1) * dv]
    state = s_in_ref[0, hd]
    sc = lax.dot_general(qh, kh, (((1,), (1,)), ((), ())), preferred_element_type=F32) * dmask_ref[hd]
    o = jnp.dot(sc.astype(BF16), vh, preferred_element_type=F32)
    o = o + jnp.dot(qh, state.astype(BF16), preferred_element_type=F32) * xi_ref[hd]
    kz = (kh.astype(F32) * zeta_ref[hd]).astype(BF16)
    s_out_ref[0, hd] = gc_ref[hd] * state + lax.dot_general(
        kz, vh, (((0,), (0,)), ((), ())), preferred_element_type=F32)
    on = _layer_norm(o, gn_ref[hd])
    return (g_ref[:, hd * dv:(hd + 1) * dv].astype(F32) * on).astype(BF16)


def _ret_kernel(qp_ref, kp_ref, vp_ref, gp_ref, qs_ref, ks_ref, vs_ref, gs_ref, s0s_ref,
                dmask_p_ref, xi_p_ref, zeta_p_ref, gc_p_ref, dmask_s_ref, xi_s_ref, zeta_s_ref, gc_s_ref, gn_ref,
                *rest):
    op_ref, os_ref, sp_ref, ss_ref = rest[-4:]

    @pl.when(pl.program_id(1) == 0)
    def _():
        sp_ref[...] = jnp.zeros(sp_ref.shape, F32)

    dv = vp_ref.shape[1] // RET_HEADS
    n_s = s0s_ref.shape[0]
    rows_s = qs_ref.shape[0] // n_s
    for hd in range(RET_HEADS):
        cols = slice(hd * dv, (hd + 1) * dv)
        op_ref[:, cols] = _ret_head(hd, qp_ref, kp_ref, vp_ref, gp_ref, sp_ref, sp_ref,
                                    dmask_p_ref, xi_p_ref, zeta_p_ref, gc_p_ref, gn_ref)
        for j in range(n_s):
            rows = pl.ds(j * rows_s, rows_s)
            os_ref[j * rows_s:(j + 1) * rows_s, cols] = _ret_head(
                hd, qs_ref.at[rows], ks_ref.at[rows], vs_ref.at[rows], gs_ref.at[rows], s0s_ref.at[j:j + 1],
                ss_ref.at[j:j + 1], dmask_s_ref, xi_s_ref, zeta_s_ref, gc_s_ref, gn_ref)


def _retention(qkvg_p, qkvg_s, s0_s_all, sp_prev, ss_prev, tabs_p, tabs_s, gn, *, layer, n_layers, n_seq_p, n_chunk,
               chunk, n_seq_s, rows_s):
    steps = n_seq_p * n_chunk
    per_step = n_seq_s // steps
    assert per_step * steps == n_seq_s
    dq, dvv = qkvg_p[0].shape[1], qkvg_p[2].shape[1]
    state_shape = (RET_HEADS, dq // RET_HEADS, dvv // RET_HEADS)
    blk_p = lambda b, c: (b * n_chunk + c, 0)
    blk_s = lambda b, c: (b * n_chunk + c, 0)
    sp_spec = pl.BlockSpec((None, 1) + state_shape, lambda b, c: (layer, b, 0, 0, 0))
    ss_spec = pl.BlockSpec((None, per_step) + state_shape, lambda b, c: (layer, b * n_chunk + c, 0, 0, 0))
    in_specs = [pl.BlockSpec((chunk, t.shape[1]), blk_p) for t in qkvg_p]
    in_specs += [pl.BlockSpec((per_step * rows_s, t.shape[1]), blk_s) for t in qkvg_s]
    in_specs.append(ss_spec)
    args = [*qkvg_p, *qkvg_s, s0_s_all]
    for t in (*tabs_p, *tabs_s, gn):
        in_specs.append(_resident(t.shape))
        args.append(t)
    aliases = {}
    for prev, out_idx in ((sp_prev, 2), (ss_prev, 3)):
        if prev is not None:
            in_specs.append(pl.BlockSpec(memory_space=pl.ANY))
            args.append(prev)
            aliases[len(args) - 1] = out_idx
    return pl.pallas_call(
        _ret_kernel,
        grid=(n_seq_p, n_chunk),
        in_specs=in_specs,
        out_specs=[pl.BlockSpec((chunk, dvv), blk_p), pl.BlockSpec((per_step * rows_s, dvv), blk_s), sp_spec, ss_spec],
        out_shape=[jax.ShapeDtypeStruct((steps * chunk, dvv), BF16),
                   jax.ShapeDtypeStruct((n_seq_s * rows_s, dvv), BF16),
                   jax.ShapeDtypeStruct((n_layers, n_seq_p) + state_shape, F32),
                   jax.ShapeDtypeStruct((n_layers, n_seq_s) + state_shape, F32)],
        input_output_aliases=aliases,
        compiler_params=_cparams(2),
        name="retention",
    )(*args)


def _decay_tables(c_true, c_pad, dk, dv):
    f32 = np.float32
    lg = np.log1p(-np.exp2(f32(-5.0) - np.arange(RET_HEADS, dtype=f32)))
    idx = np.arange(c_true, dtype=f32)
    diff = idx[:, None] - idx[None, :]
    dmask = np.where(diff >= 0, np.exp(lg[:, None, None] * np.maximum(diff, f32(0.0))), f32(0.0))
    xi = np.exp(lg[:, None] * (idx + f32(1.0)))
    zeta = np.exp(lg[:, None] * (f32(c_true - 1.0) - idx))
    g_c = np.exp(lg * f32(c_true))
    pad = c_pad - c_true
    dmask = np.pad(dmask, ((0, 0), (0, pad), (0, pad)))
    xi = np.pad(xi, ((0, 0), (0, pad)))
    zeta = np.pad(zeta, ((0, 0), (0, pad)))
    tabs = (dmask,
            np.broadcast_to(xi[:, :, None], (RET_HEADS, c_pad, dv)),
            np.broadcast_to(zeta[:, :, None], (RET_HEADS, c_pad, dk)),
            np.broadcast_to(g_c[:, None, None], (RET_HEADS, 1, dv)))
    return tuple(jnp.asarray(t, dtype=F32) for t in tabs)


def kernel(x_prompt, x_sample, state_pool, state_ret, state_conv, w_mix_in, w_pool_grp, pool_scale, w_spatial,
           b_spatial, sgu_norm_g, sgu_norm_b, w_mix_out, w_q, w_k, w_v, w_g, ret_norm_g, w_ret_out, norm_mix_pre,
           norm_mix_post, norm_ffn_pre, norm_ffn_post, w_ffn_gate, w_ffn_up, w_dconv, b_dconv, w_ffn_down):
    bp, seq, d = x_prompt.shape
    bs, dec_seq, _ = x_sample.shape
    depth = norm_mix_pre.shape[0]
    n_ret = w_q.shape[0]
    a_width = w_pool_grp.shape[1] * w_pool_grp.shape[2]
    b_width = sgu_norm_g.shape[1]
    dk = w_q.shape[2] // RET_HEADS
    dv = w_v.shape[2] // RET_HEADS
    d_ff = w_ffn_gate.shape[2]
    m_p, m_s = bp * seq, bs * dec_seq
    m = m_p + m_s
    assert m_s == TM and seq % TM_FFN == 0 and seq % RET_CHUNK == 0 and seq >= POOL_BUF and a_width == b_width
    assert CONV_W - 1 <= dec_seq < POOL_BUF and dec_seq <= SGU_CHUNK and dec_seq <= SUBLANES
    n_prompt_tiles = m_p // TM
    tiles_per_seq = seq // TM
    geom = dict(n_prompt_tiles=n_prompt_tiles, tiles_per_seq=tiles_per_seq, dec_batch=bs)

    def to_rows(t):
        return t.transpose(1, 0, 2).reshape(t.shape[1] * bs, t.shape[-1])

    def from_rows(r, steps):
        return r.reshape(steps, bs, r.shape[-1]).transpose(1, 0, 2)

    x = (x_prompt.reshape(m_p, d), to_rows(x_sample))

    half = dk // 2
    inv = np.float32(ROPE_BASE) ** (-np.arange(half, dtype=np.float32) / np.float32(half))
    pos_p = np.arange(seq).astype(np.float32)
    pos_s = (PAST_LEN + np.arange(dec_seq)).astype(np.float32)
    ang = np.concatenate([np.tile(pos_p[:, None] * inv[None, :], (bp, 1)),
                          np.repeat(pos_s[:, None] * inv[None, :], bs, axis=0)], axis=0)
    rot = (jnp.asarray(np.cos(ang), dtype=F32), jnp.asarray(np.sin(ang), dtype=F32))
    rot_specs = (pl.BlockSpec((TM_PROJ, half), lambda j, i: (i, 0)),) * 2

    pos = np.arange(seq)
    invc = jnp.asarray(np.concatenate(
        [np.broadcast_to((np.float32(1.0) / np.minimum(pos + 1, w).astype(np.float32))[:, None], (seq, LANES))
         for w in POOL_WINDOWS], axis=1), dtype=F32)

    tril_p = np.tril(np.ones((SGU_CHUNK, SGU_CHUNK), dtype=bool))
    tril_s = np.tril(np.ones((dec_seq, dec_seq), dtype=bool))
    eye_b = np.eye(bs, dtype=np.float32)
    hdim = b_width // SGU_HEADS

    dec_pad = SUBLANES
    tabs_p = _decay_tables(RET_CHUNK, RET_CHUNK, dk, dv)
    tabs_s = _decay_tables(dec_seq, dec_pad, dk, dv)

    def pad_steps(r):
        t = from_rows(r, dec_seq)
        return jnp.pad(t, ((0, 0), (0, dec_pad - dec_seq), (0, 0))).reshape(bs * dec_pad, r.shape[-1])

    conv0_p = jnp.zeros((SUBLANES, d_ff), F32)
    pool_p, pool_s, vn_s, conv_p, conv_s = [], [], [], [], []
    ret_p = ret_s = None
    h = _norm(*x, norm_mix_pre[0])
    for l in range(depth):
        if l % 2 == 0:
            e = l // 2
            pw = dict(layer=e, tn=a_width, n_out=a_width)
            a, w_o = _proj(h, w_mix_in, col0=0, out_dtype=F32, epilogue="none", name="proj_a", side=(w_mix_out, e),
                           **pw)
            u = _proj(h, w_mix_in, col0=1, out_dtype=BF16, epilogue="gelu", name="proj_u", **pw)
            vn = _proj(h, w_mix_in, col0=2, out_dtype=F32, epilogue="gelu_ln",
                       extras=(sgu_norm_g[e].reshape(1, b_width), sgu_norm_b[e].reshape(1, b_width)),
                       extra_specs=(pl.BlockSpec((1, b_width), lambda j, i: (0, 0)),) * 2, name="proj_v", **pw)
            ws_p = jnp.where(tril_p, w_spatial[e][:, :SGU_CHUNK, :SGU_CHUNK], 0.0)
            ws_s = jnp.where(tril_s, w_spatial[e][:, :dec_seq, :dec_seq], 0.0)
            wmix_s = jnp.einsum("hij,bc->hibjc", ws_s, eye_b).reshape(SGU_HEADS, m_s, m_s)
            bias_p = jnp.broadcast_to(b_spatial[e][:, :SGU_CHUNK, None], (SGU_HEADS, SGU_CHUNK, hdim))
            bias_s = jnp.broadcast_to(b_spatial[e][:, :dec_seq, None, None],
                                      (SGU_HEADS, dec_seq, bs, hdim)).reshape(SGU_HEADS, m_s, hdim)
            mixed, pool_new = _mix(a, u, vn, invc, state_pool, w_pool_grp[e].astype(BF16),
                                   pool_scale[e].reshape(1, a_width), ws_p.astype(BF16), bias_p, wmix_s.astype(BF16),
                                   bias_s, layer=e, **geom)
            pool_p.append(jnp.stack([a[(b + 1) * seq - POOL_BUF:(b + 1) * seq] for b in range(bp)]))
            pool_s.append(pool_new)
            vn_s.append(from_rows(vn[m_p:], dec_seq))
        else:
            r = l // 2
            pw = dict(layer=r, col0=0, tn=1024, out_dtype=BF16)
            q = _proj(h, w_q, n_out=w_q.shape[2], epilogue="rotary", extras=rot, extra_specs=rot_specs,
                      name="proj_q", **pw)
            k = _proj(h, w_k, n_out=w_k.shape[2], epilogue="rotary", extras=rot, extra_specs=rot_specs,
                      scale=dk ** -0.5, name="proj_k", **pw)
            v = _proj(h, w_v, n_out=w_v.shape[2], epilogue="none", name="proj_v_ret", **pw)
            g, w_o = _proj(h, w_g, n_out=w_g.shape[2], epilogue="silu", name="proj_g", side=(w_ret_out, r), **pw)
            gn = ret_norm_g[r].reshape(RET_HEADS, 1, dv)
            qkvg = (q, k, v, g)
            gated, gated_s, ret_p, ret_s = _retention(
                qkvg, tuple(pad_steps(t[m_p:]) for t in qkvg), state_ret, ret_p, ret_s, tabs_p, tabs_s, gn, layer=r,
                n_layers=n_ret, n_seq_p=bp, n_chunk=seq // RET_CHUNK, chunk=RET_CHUNK, n_seq_s=bs, rows_s=dec_pad)
            gated_s = to_rows(gated_s.reshape(bs, dec_pad, -1)[:, :dec_seq])
            mixed = (gated, gated_s)
        x, h = _out_proj(mixed, w_o, x, norm_mix_post[l], norm_ffn_pre[l], tm=TM_OUT, m_first=m_p, name="mix_out")
        wc, bc = w_dconv, b_dconv.reshape(depth, 1, d_ff)
        act_p, tail_p, wgb, wub, wdb = _ffn_in_prompt(h, w_ffn_gate, w_ffn_up, wc, bc, conv0_p, w_ffn_down, layer=l,
                                                      tm=TM_FFN, tn=TN_FFN, n_tiles=m_p // TM_FFN,
                                                      tiles_per_seq=seq // TM_FFN)
        act_s, conv_new = _ffn_in_sample(h, wgb, wub, wc, bc, state_conv, layer=l, tm=m_s, tn=TN_FFN,
                                         row0=m_p // m_s)
        conv_p.append(tail_p.reshape(bp, seq // TM_FFN, SUBLANES, d_ff)[:, -1, SUBLANES - (CONV_W - 1):])
        conv_s.append(conv_new)
        g_next = norm_mix_pre[l + 1] if l + 1 < depth else None
        x, h = _out_proj((act_p, act_s), wdb, x, norm_ffn_post[l], g_next, tm=TM_DOWN, m_first=m_p, name="ffn_out")

    y_prompt = x[0].reshape(bp, seq, d)
    y_sample = from_rows(x[1], dec_seq)
    return (y_prompt, y_sample, jnp.stack(pool_p), jnp.stack(pool_s), jnp.stack(vn_s),
            ret_p, ret_s, jnp.stack(conv_p), jnp.stack(conv_s))
```

```python
import functools

import jax
import jax.numpy as jnp
import numpy as np
from jax import lax
from jax.experimental import pallas as pl
from jax.experimental.pallas import tpu as pltpu

F32 = jnp.float32
BF16 = jnp.bfloat16

PAST_LEN = 16384
POOL_WINDOWS = (2, 4, 8, 16)
POOL_BUF = max(POOL_WINDOWS) - 1
SGU_HEADS = 4
SGU_CHUNK = 128
RET_HEADS = 8
RET_CHUNK = 128
ROPE_BASE = 10000.0
CONV_W = 3
EPS = 1e-6

V7X_VMEM_BYTES = 64 * 1024 * 1024
VMEM_LIMIT_BYTES = V7X_VMEM_BYTES - 4 * 1024 * 1024
SUBLANES = 8
LANES = 128
BF16_SUBLANES = 16

TM = 512
TM_PROJ = 1088
TM_FFN = 1024
TN_FFN = 512
TM_OUT = 256
TM_DOWN = 256


def _cparams(n_axes):
    return pltpu.CompilerParams(
        dimension_semantics=("arbitrary",) * n_axes, vmem_limit_bytes=VMEM_LIMIT_BYTES)


def _resident(shape):
    zeros = (0,) * len(shape)
    return pl.BlockSpec(shape, lambda *_: zeros, pipeline_mode=pl.Buffered(1))


def _rms(x, g):
    return x * lax.rsqrt(jnp.mean(x * x, axis=-1, keepdims=True) + EPS) * g


def _layer_norm(x, g):
    mu = jnp.mean(x, axis=-1, keepdims=True)
    xc = x - mu
    return xc * lax.rsqrt(jnp.mean(xc * xc, axis=-1, keepdims=True) + EPS) * g


def _first_spec(tm, d, n_first):
    return pl.BlockSpec((tm, d), lambda i: (jnp.minimum(i, n_first - 1), 0))


def _second_spec(tm, d, n_first):
    return pl.BlockSpec((tm, d), lambda i: (jnp.maximum(i - n_first, 0), 0))


def _norm_kernel(xp_ref, xs_ref, g_ref, h_ref, *, n_first):
    i = pl.program_id(0)

    @pl.when(i < n_first)
    def _():
        h_ref[...] = _rms(xp_ref[...], g_ref[...]).astype(BF16)

    @pl.when(i >= n_first)
    def _():
        h_ref[...] = _rms(xs_ref[...], g_ref[...]).astype(BF16)


def _norm(xp, xs, g):
    d = xp.shape[1]
    m = xp.shape[0] + xs.shape[0]
    n_first = xp.shape[0] // TM
    return pl.pallas_call(
        functools.partial(_norm_kernel, n_first=n_first),
        grid=(m // TM,),
        in_specs=[_first_spec(TM, d, n_first), _second_spec(TM, d, n_first), _resident((1, d))],
        out_specs=pl.BlockSpec((TM, d), lambda i: (i, 0)),
        out_shape=jax.ShapeDtypeStruct((m, d), BF16),
        compiler_params=_cparams(1),
        name="norm",
    )(xp, xs, g.reshape(1, d))


def _proj_kernel(h_ref, w_ref, *rest, epilogue, scale, side_cast):
    if side_cast:
        *extras, side_ref, o_ref, side_out_ref, wb_ref = rest
        side_out_ref[...] = side_ref[...].astype(BF16)
    elif epilogue == "gelu_ln":
        *extras, o_ref, last_f32_ref, wb_ref = rest
    else:
        *extras, o_ref, wb_ref = rest

    @pl.when(pl.program_id(1) == 0)
    def _():
        wb_ref[...] = w_ref[...].astype(BF16)

    z = jnp.dot(h_ref[...], wb_ref[...], preferred_element_type=F32)
    if epilogue == "none":
        o_ref[...] = z.astype(o_ref.dtype)
    elif epilogue == "gelu":
        o_ref[...] = jax.nn.gelu(z, approximate=True).astype(o_ref.dtype)
    elif epilogue == "silu":
        o_ref[...] = jax.nn.silu(z).astype(o_ref.dtype)
    elif epilogue == "gelu_ln":
        g_ref, b_ref = extras
        vn = _layer_norm(jax.nn.gelu(z, approximate=True), g_ref[...]) + b_ref[...]
        o_ref[...] = vn.astype(o_ref.dtype)
        last_f32_ref[...] = vn
    elif epilogue == "rotary":
        cos_ref, sin_ref = extras
        c = cos_ref[...]
        s = sin_ref[...]
        half = c.shape[-1]
        for hd in range(z.shape[-1] // (2 * half)):
            lo = hd * 2 * half
            x1 = z[:, lo:lo + half]
            x2 = z[:, lo + half:lo + 2 * half]
            o_ref[:, lo:lo + half] = ((x1 * c - x2 * s) * scale).astype(o_ref.dtype)
            o_ref[:, lo + half:lo + 2 * half] = ((x1 * s + x2 * c) * scale).astype(o_ref.dtype)
    else:
        raise ValueError(epilogue)


def _proj(h, w, *, layer, col0, n_out, tn, out_dtype, epilogue, extras=(), extra_specs=(), scale=1.0, name,
          side=None):
    m, k = h.shape
    assert m % TM_PROJ == 0 and n_out % tn == 0
    n_i = m // TM_PROJ
    in_specs = [pl.BlockSpec((TM_PROJ, k), lambda j, i: (i, 0)),
                pl.BlockSpec((None, k, tn), lambda j, i: (layer, 0, col0 + j)), *extra_specs]
    args = [h, w, *extras]
    out_specs = [pl.BlockSpec((TM_PROJ, tn), lambda j, i: (i, j))]
    out_shape = [jax.ShapeDtypeStruct((m, n_out), out_dtype)]
    if side is not None:
        w2, layer2 = side
        rows = w2.shape[1] // ((n_out // tn) * n_i)
        assert rows * (n_out // tn) * n_i == w2.shape[1] and rows % BF16_SUBLANES == 0
        in_specs.append(pl.BlockSpec((None, rows, w2.shape[2]), lambda j, i: (layer2, j * n_i + i, 0)))
        args.append(w2)
        out_specs.append(pl.BlockSpec((rows, w2.shape[2]), lambda j, i: (j * n_i + i, 0)))
        out_shape.append(jax.ShapeDtypeStruct(w2.shape[1:], BF16))
    if epilogue == "gelu_ln":
        assert side is None
        out_specs.append(pl.BlockSpec((TM_PROJ, tn), lambda j, i: (0, j)))
        out_shape.append(jax.ShapeDtypeStruct((TM_PROJ, n_out), F32))
    res = pl.pallas_call(
        functools.partial(_proj_kernel, epilogue=epilogue, scale=scale, side_cast=side is not None),
        grid=(n_out // tn, n_i),
        in_specs=in_specs,
        out_specs=out_specs,
        out_shape=out_shape,
        scratch_shapes=[pltpu.VMEM((k, tn), BF16)],
        compiler_params=_cparams(2),
        name=name,
    )(*args)
    return res[0] if len(res) == 1 else tuple(res)


def _out_kernel(*refs, n_first, split_lhs, split_x, last):
    refs = list(refs)
    lhs_refs = [refs.pop(0) for _ in range(2 if split_lhs else 1)]
    wb_ref = refs.pop(0)
    x_refs = [refs.pop(0) for _ in range(2 if split_x else 1)]
    gpost_ref, gnext_ref, *outs = refs
    i = pl.program_id(0)

    def pick(pair):
        return jnp.where(i < n_first, pair[0][...], pair[1][...]) if len(pair) == 2 else pair[0][...]

    y = jnp.dot(pick(lhs_refs), wb_ref[...], preferred_element_type=F32)
    x_new = pick(x_refs) + _rms(y, gpost_ref[...])
    if last:
        yp_ref, ys_ref = outs

        @pl.when(i < n_first)
        def _():
            yp_ref[...] = x_new

        @pl.when(i >= n_first)
        def _():
            ys_ref[...] = x_new
    else:
        xo_ref, h_ref = outs
        xo_ref[...] = x_new
        h_ref[...] = _rms(x_new, gnext_ref[...]).astype(BF16)


def _out_proj(lhs, w, x, g_post, g_next, *, tm, m_first, name):
    k, d = w.shape
    n_first = m_first // tm
    split_lhs, split_x = isinstance(lhs, tuple), isinstance(x, tuple)
    m = sum(t.shape[0] for t in lhs) if split_lhs else lhs.shape[0]
    last = g_next is None

    def pair(width):
        return [_first_spec(tm, width, n_first), _second_spec(tm, width, n_first)]

    def operand(t, width):
        return (list(t), pair(width)) if isinstance(t, tuple) else ([t], [pl.BlockSpec((tm, width), lambda i: (i, 0))])

    lhs_args, lhs_specs = operand(lhs, k)
    x_args, x_specs = operand(x, d)
    if last:
        g_next = g_post
        out_specs = pair(d)
        out_shape = [jax.ShapeDtypeStruct((m_first, d), F32), jax.ShapeDtypeStruct((m - m_first, d), F32)]
    else:
        out_specs = [pl.BlockSpec((tm, d), lambda i: (i, 0))] * 2
        out_shape = [jax.ShapeDtypeStruct((m, d), F32), jax.ShapeDtypeStruct((m, d), BF16)]
    res = pl.pallas_call(
        functools.partial(_out_kernel, n_first=n_first, split_lhs=split_lhs, split_x=split_x, last=last),
        grid=(m // tm,),
        in_specs=[*lhs_specs, _resident((k, d)), *x_specs, _resident((1, d)), _resident((1, d))],
        out_specs=out_specs,
        out_shape=out_shape,
        compiler_params=_cparams(1),
        name=name,
    )(*lhs_args, w, *x_args, g_post.reshape(1, d), g_next.reshape(1, d))
    return ((res[0], res[1]), None) if last else (res[0], res[1])


def _mix_group(a_ref, u_ref, vn_ref, wgrp_ref, pscale_ref, wmix_ref, bias_ref, o_ref, ext_ref,
               inv_cnt, halo_rows, shift, chunk):
    tm, a_width = a_ref.shape
    gdim = a_width // len(POOL_WINDOWS)
    ext_ref[halo_rows:halo_rows + tm, :] = a_ref[...]
    if shift == 1:
        assert POOL_WINDOWS == tuple(2 ** (gi + 1) for gi in range(len(POOL_WINDOWS))) and halo_rows > POOL_BUF
        sums, cur = [], ext_ref[0:halo_rows + tm, :]
        for gi in range(len(POOL_WINDOWS)):
            cur = cur + pltpu.roll(cur, 2 ** gi, axis=0)
            sums.append(cur[halo_rows:halo_rows + tm, 0:gdim])
            cur = cur[:, gdim:]
    else:
        sums = []
        for gi, w in enumerate(POOL_WINDOWS):
            s = ext_ref[halo_rows:halo_rows + tm, gi * gdim:(gi + 1) * gdim]
            for j in range(1, w):
                s = s + ext_ref[halo_rows - j * shift:halo_rows - j * shift + tm, gi * gdim:(gi + 1) * gdim]
            sums.append(s)
    for gi, s in enumerate(sums):
        c0, c1 = gi * gdim, (gi + 1) * gdim
        d = (s * inv_cnt(gi) - a_ref[:, c0:c1]).astype(BF16)
        z = jnp.dot(d, wgrp_ref[gi], preferred_element_type=F32)
        o_ref[:, c0:c1] = (z * pscale_ref[:, c0:c1]).astype(o_ref.dtype)
    hdim = vn_ref.shape[1] // SGU_HEADS
    for c in range(tm // chunk):
        r0, r1 = c * chunk, (c + 1) * chunk
        for hd in range(SGU_HEADS):
            c0, c1 = hd * hdim, (hd + 1) * hdim
            mixed = jnp.dot(wmix_ref[hd], vn_ref[r0:r1, c0:c1].astype(BF16),
                            preferred_element_type=F32) + bias_ref[hd]
            o_ref[r0:r1, a_width + c0:a_width + c1] = (
                u_ref[r0:r1, c0:c1].astype(F32) * mixed).astype(o_ref.dtype)


def _mix_kernel(a_ref, u_ref, vn_ref, invc_ref, pstate_ref, wgrp_ref, pscale_ref,
                wmix_p_ref, bias_p_ref, wmix_s_ref, bias_s_ref, o_ref, pnew_ref, ext_ref,
                *, n_prompt_tiles, tiles_per_seq, dec_batch):
    i = pl.program_id(0)
    tm = a_ref.shape[0]
    gl = LANES
    halo_p = 2 * SUBLANES

    @pl.when(i < n_prompt_tiles)
    def _prompt():
        @pl.when(i % tiles_per_seq == 0)
        def _():
            ext_ref[0:halo_p, :] = jnp.zeros((halo_p, ext_ref.shape[1]), F32)

        def inv_cnt(gi):
            blk = invc_ref[:, gi * gl:(gi + 1) * gl]
            return jnp.concatenate([blk, blk], axis=1)

        _mix_group(a_ref, u_ref, vn_ref, wgrp_ref, pscale_ref, wmix_p_ref, bias_p_ref, o_ref, ext_ref,
                   inv_cnt, halo_p, 1, SGU_CHUNK)
        ext_ref[0:halo_p, :] = ext_ref[tm:tm + halo_p, :]

    @pl.when(i >= n_prompt_tiles)
    def _sample():
        halo_s = POOL_BUF * dec_batch
        for t in range(POOL_BUF):
            ext_ref[t * dec_batch:(t + 1) * dec_batch, :] = pstate_ref[:, t, :]
        _mix_group(a_ref, u_ref, vn_ref, wgrp_ref, pscale_ref, wmix_s_ref, bias_s_ref, o_ref, ext_ref,
                   lambda gi: 1.0 / POOL_WINDOWS[gi], halo_s, dec_batch, tm)
        for t in range(POOL_BUF):
            src = tm + t * dec_batch
            pnew_ref[:, t, :] = ext_ref[src:src + dec_batch, :]


def _mix(a, u, vn, invc, pstate_all, wgrp, pscale, wmix_p, bias_p, wmix_s, bias_s, *, layer, n_prompt_tiles,
         tiles_per_seq, dec_batch):
    m, a_width = a.shape
    pshape = pstate_all.shape[1:]
    b_width = u.shape[1]
    row = lambda i: (i, 0)
    ext_rows = max(2 * SUBLANES, POOL_BUF * dec_batch) + TM
    return pl.pallas_call(
        functools.partial(_mix_kernel, n_prompt_tiles=n_prompt_tiles, tiles_per_seq=tiles_per_seq,
                          dec_batch=dec_batch),
        grid=(m // TM,),
        in_specs=[pl.BlockSpec((TM, a_width), row), pl.BlockSpec((TM, b_width), row),
                  pl.BlockSpec((TM, b_width), row),
                  pl.BlockSpec((TM, invc.shape[1]), lambda i: (jnp.minimum(i, n_prompt_tiles - 1) % tiles_per_seq, 0)),
                  pl.BlockSpec((None,) + pshape, lambda i: (layer, 0, 0, 0), pipeline_mode=pl.Buffered(1)),
                  _resident(wgrp.shape), _resident(pscale.shape),
                  _resident(wmix_p.shape), _resident(bias_p.shape), _resident(wmix_s.shape),
                  _resident(bias_s.shape)],
        out_specs=[pl.BlockSpec((TM, a_width + b_width), row), pl.BlockSpec(pshape, lambda i: (0, 0, 0))],
        out_shape=[jax.ShapeDtypeStruct((m, a_width + b_width), BF16), jax.ShapeDtypeStruct(pshape, F32)],
        scratch_shapes=[pltpu.VMEM((ext_rows, a_width), F32)],
        compiler_params=_cparams(1),
        name="mix",
    )(a, u, vn, invc, pstate_all, wgrp, pscale, wmix_p, bias_p, wmix_s, bias_s)


def _conv_gelu_gate(gate, prev2, prev1, up, wc_ref, bc_ref):
    conv = bc_ref[...] + prev2 * wc_ref[0:1, :]
    conv = conv + prev1 * wc_ref[1:2, :]
    conv = conv + gate * wc_ref[2:3, :]
    return (jax.nn.gelu(conv, approximate=True) * up).astype(BF16)


def _ffn_in_prompt_kernel(h_ref, wg_ref, wu_ref, wc_ref, bc_ref, cstate_ref, wd_ref, act_ref, tail_ref, wgb_ref,
                          wub_ref, wdb_ref, halo_ref, *, tiles_per_seq):
    i = pl.program_id(1)
    wdb_ref[...] = wd_ref[...].astype(BF16)
    tm = h_ref.shape[0]
    halo = cstate_ref.shape[0]

    @pl.when(i == 0)
    def _():
        wgb_ref[...] = wg_ref[...].astype(BF16)
        wub_ref[...] = wu_ref[...].astype(BF16)
        halo_ref[...] = cstate_ref[...]

    h = h_ref[...]
    gate = jnp.dot(h, wgb_ref[...], preferred_element_type=F32)
    up = jnp.dot(h, wub_ref[...], preferred_element_type=F32)
    before = jnp.where((i % tiles_per_seq) == 0, cstate_ref[...], halo_ref[...])
    row = lax.broadcasted_iota(jnp.int32, before.shape, 0)

    def shifted(s):
        r = pltpu.roll(gate, s, axis=0)
        first = jnp.where(row < s, pltpu.roll(before, s, axis=0), r[0:halo])
        return jnp.concatenate([first, r[halo:]], axis=0)

    act_ref[...] = _conv_gelu_gate(gate, shifted(2), shifted(1), up, wc_ref, bc_ref)
    tail = gate[tm - halo:tm, :]
    tail_ref[...] = tail
    halo_ref[...] = tail


def _ffn_in_sample_kernel(h_ref, wgb_ref, wub_ref, wc_ref, bc_ref, cstate_ref, act_ref, cnew_ref, ext_ref):
    tm = h_ref.shape[0]
    shift, n_state = cstate_ref.shape[0], cstate_ref.shape[1]
    halo = n_state * shift
    h = h_ref[...]
    gate = jnp.dot(h, wgb_ref[...], preferred_element_type=F32)
    up = jnp.dot(h, wub_ref[...], preferred_element_type=F32)
    for t in range(n_state):
        ext_ref[t * shift:(t + 1) * shift, :] = cstate_ref[:, t, :]
    ext_ref[halo:halo + tm, :] = gate
    act_ref[...] = _conv_gelu_gate(gate, ext_ref[halo - 2 * shift:halo - 2 * shift + tm, :],
                                   ext_ref[halo - shift:halo - shift + tm, :], up, wc_ref, bc_ref)
    for t in range(n_state):
        cnew_ref[:, t, :] = gate[tm - halo + t * shift:tm - halo + (t + 1) * shift, :]


def _ffn_in_prompt(h, wg, wu, wc, bc, cstate, wd, *, layer, tm, tn, n_tiles, tiles_per_seq):
    k = h.shape[1]
    n = wg.shape[2]
    halo = cstate.shape[0]
    steps = (n // tn) * n_tiles
    wd_rows = wd.shape[1] // steps
    assert wd_rows * steps == wd.shape[1] and wd_rows % BF16_SUBLANES == 0
    d_out = wd.shape[2]
    wspec = pl.BlockSpec((None, k, tn), lambda j, i: (layer, 0, j))
    wbspec = pl.BlockSpec((k, tn), lambda j, i: (0, j))
    return pl.pallas_call(
        functools.partial(_ffn_in_prompt_kernel, tiles_per_seq=tiles_per_seq),
        grid=(n // tn, n_tiles),
        in_specs=[pl.BlockSpec((tm, k), lambda j, i: (i, 0)), wspec, wspec,
                  pl.BlockSpec((None, CONV_W, tn), lambda j, i: (layer, 0, j)),
                  pl.BlockSpec((None, 1, tn), lambda j, i: (layer, 0, j)),
                  pl.BlockSpec((halo, tn), lambda j, i: (0, j)),
                  pl.BlockSpec((None, wd_rows, d_out), lambda j, i: (layer, j * n_tiles + i, 0))],
        out_specs=[pl.BlockSpec((tm, tn), lambda j, i: (i, j)), pl.BlockSpec((halo, tn), lambda j, i: (i, j)),
                   wbspec, wbspec, pl.BlockSpec((wd_rows, d_out), lambda j, i: (j * n_tiles + i, 0))],
        out_shape=[jax.ShapeDtypeStruct((n_tiles * tm, n), BF16), jax.ShapeDtypeStruct((n_tiles * halo, n), F32),
                   jax.ShapeDtypeStruct((k, n), BF16), jax.ShapeDtypeStruct((k, n), BF16),
                   jax.ShapeDtypeStruct(wd.shape[1:], BF16)],
        scratch_shapes=[pltpu.VMEM((halo, tn), F32)],
        compiler_params=_cparams(2),
        name="ffn_in_prompt",
    )(h, wg, wu, wc, bc, cstate, wd)


def _ffn_in_sample(h, wgb, wub, wc, bc, cstate_all, *, layer, tm, tn, row0):
    k = h.shape[1]
    n = wgb.shape[1]
    bs, n_state = cstate_all.shape[1:3]
    wbspec = pl.BlockSpec((k, tn), lambda j: (0, j))
    return pl.pallas_call(
        _ffn_in_sample_kernel,
        grid=(n // tn,),
        in_specs=[pl.BlockSpec((tm, k), lambda j: (row0, 0)), wbspec, wbspec,
                  pl.BlockSpec((None, CONV_W, tn), lambda j: (layer, 0, j)),
                  pl.BlockSpec((None, 1, tn), lambda j: (layer, 0, j)),
                  pl.BlockSpec((None, bs, n_state, tn), lambda j: (layer, 0, 0, j))],
        out_specs=[pl.BlockSpec((tm, tn), lambda j: (0, j)), pl.BlockSpec((bs, n_state, tn), lambda j: (0, 0, j))],
        out_shape=[jax.ShapeDtypeStruct((tm, n), BF16), jax.ShapeDtypeStruct((bs, n_state, n), F32)],
        scratch_shapes=[pltpu.VMEM((n_state * bs + tm, tn), F32)],
        compiler_params=_cparams(1),
        name="ffn_in_sample",
    )(h, wgb, wub, wc, bc, cstate_all)


def _ret_head(hd, q_ref, k_ref, v_ref, g_ref, s_in_ref, s_out_ref, dmask_ref, xi_ref, zeta_ref, gc_ref, gn_ref):
    dk = q_ref.shape[1] // RET_HEADS
    dv = v_ref.shape[1] // RET_HEADS
    qh = q_ref[:, hd * dk:(hd + 1) * dk]
    kh = k_ref[:, hd * dk:(hd + 1) * dk]
    vh = v_ref[:, hd * dv:(hd + 1) * dv]
    state = s_in_ref[0, hd]
    sc = lax.dot_general(qh, kh, (((1,), (1,)), ((), ())), preferred_element_type=F32) * dmask_ref[hd]
    o = jnp.dot(sc.astype(BF16), vh, preferred_element_type=F32)
    o = o + jnp.dot(qh, state.astype(BF16), preferred_element_type=F32) * xi_ref[hd]
    kz = (kh.astype(F32) * zeta_ref[hd]).astype(BF16)
    s_out_ref[0, hd] = gc_ref[hd] * state + lax.dot_general(
        kz, vh, (((0,), (0,)), ((), ())), preferred_element_type=F32)
    on = _layer_norm(o, gn_ref[hd])
    return (g_ref[:, hd * dv:(hd + 1) * dv].astype(F32) * on).astype(BF16)


def _ret_kernel(qp_ref, kp_ref, vp_ref, gp_ref, qs_ref, ks_ref, vs_ref, gs_ref, s0s_ref,
                dmask_p_ref, xi_p_ref, zeta_p_ref, gc_p_ref, dmask_s_ref, xi_s_ref, zeta_s_ref, gc_s_ref, gn_ref,
                *rest):
    op_ref, os_ref, sp_ref, ss_ref = rest[-4:]

    @pl.when(pl.program_id(1) == 0)
    def _():
        sp_ref[...] = jnp.zeros(sp_ref.shape, F32)

    dv = vp_ref.shape[1] // RET_HEADS
    n_s = s0s_ref.shape[0]
    rows_s = qs_ref.shape[0] // n_s
    for hd in range(RET_HEADS):
        cols = slice(hd * dv, (hd + 1) * dv)
        op_ref[:, cols] = _ret_head(hd, qp_ref, kp_ref, vp_ref, gp_ref, sp_ref, sp_ref,
                                    dmask_p_ref, xi_p_ref, zeta_p_ref, gc_p_ref, gn_ref)
        for j in range(n_s):
            rows = pl.ds(j * rows_s, rows_s)
            os_ref[j * rows_s:(j + 1) * rows_s, cols] = _ret_head(
                hd, qs_ref.at[rows], ks_ref.at[rows], vs_ref.at[rows], gs_ref.at[rows], s0s_ref.at[j:j + 1],
                ss_ref.at[j:j + 1], dmask_s_ref, xi_s_ref, zeta_s_ref, gc_s_ref, gn_ref)


def _retention(qkvg_p, qkvg_s, s0_s_all, sp_prev, ss_prev, tabs_p, tabs_s, gn, *, layer, n_layers, n_seq_p, n_chunk,
               chunk, n_seq_s, rows_s):
    steps = n_seq_p * n_chunk
    per_step = n_seq_s // steps
    assert per_step * steps == n_seq_s
    dq, dvv = qkvg_p[0].shape[1], qkvg_p[2].shape[1]
    state_shape = (RET_HEADS, dq // RET_HEADS, dvv // RET_HEADS)
    blk_p = blk_s = lambda b, c: (b * n_chunk + c, 0)
    sp_spec = pl.BlockSpec((None, 1) + state_shape, lambda b, c: (layer, b, 0, 0, 0))
    ss_spec = pl.BlockSpec((None, per_step) + state_shape, lambda b, c: (layer, b * n_chunk + c, 0, 0, 0))
    in_specs = [pl.BlockSpec((chunk, t.shape[1]), blk_p) for t in qkvg_p]
    in_specs += [pl.BlockSpec((per_step * rows_s, t.shape[1]), blk_s) for t in qkvg_s]
    in_specs.append(ss_spec)
    args = [*qkvg_p, *qkvg_s, s0_s_all]
    for t in (*tabs_p, *tabs_s, gn):
        in_specs.append(_resident(t.shape))
        args.append(t)
    aliases = {}
    for prev, out_idx in ((sp_prev, 2), (ss_prev, 3)):
        if prev is not None:
            in_specs.append(pl.BlockSpec(memory_space=pl.ANY))
            args.append(prev)
            aliases[len(args) - 1] = out_idx
    return pl.pallas_call(
        _ret_kernel,
        grid=(n_seq_p, n_chunk),
        in_specs=in_specs,
        out_specs=[pl.BlockSpec((chunk, dvv), blk_p), pl.BlockSpec((per_step * rows_s, dvv), blk_s), sp_spec, ss_spec],
        out_shape=[jax.ShapeDtypeStruct((steps * chunk, dvv), BF16),
                   jax.ShapeDtypeStruct((n_seq_s * rows_s, dvv), BF16),
                   jax.ShapeDtypeStruct((n_layers, n_seq_p) + state_shape, F32),
                   jax.ShapeDtypeStruct((n_layers, n_seq_s) + state_shape, F32)],
        input_output_aliases=aliases,
        compiler_params=_cparams(2),
        name="retention",
    )(*args)


def _decay_tables(c_true, c_pad, dk, dv):
    f32 = np.float32
    lg = np.log1p(-np.exp2(f32(-5.0) - np.arange(RET_HEADS, dtype=f32)))
    idx = np.arange(c_true, dtype=f32)
    diff = idx[:, None] - idx[None, :]
    dmask = np.where(diff >= 0, np.exp(lg[:, None, None] * np.maximum(diff, f32(0.0))), f32(0.0))
    xi = np.exp(lg[:, None] * (idx + f32(1.0)))
    zeta = np.exp(lg[:, None] * (f32(c_true - 1.0) - idx))
    g_c = np.exp(lg * f32(c_true))
    pad = c_pad - c_true
    dmask = np.pad(dmask, ((0, 0), (0, pad), (0, pad)))
    xi = np.pad(xi, ((0, 0), (0, pad)))
    zeta = np.pad(zeta, ((0, 0), (0, pad)))
    tabs = (dmask,
            np.broadcast_to(xi[:, :, None], (RET_HEADS, c_pad, dv)),
            np.broadcast_to(zeta[:, :, None], (RET_HEADS, c_pad, dk)),
            np.broadcast_to(g_c[:, None, None], (RET_HEADS, 1, dv)))
    return tuple(jnp.asarray(t, dtype=F32) for t in tabs)


def kernel(x_prompt, x_sample, state_pool, state_ret, state_conv, w_mix_in, w_pool_grp, pool_scale, w_spatial,
           b_spatial, sgu_norm_g, sgu_norm_b, w_mix_out, w_q, w_k, w_v, w_g, ret_norm_g, w_ret_out, norm_mix_pre,
           norm_mix_post, norm_ffn_pre, norm_ffn_post, w_ffn_gate, w_ffn_up, w_dconv, b_dconv, w_ffn_down):
    bp, seq, d = x_prompt.shape
    bs, dec_seq, _ = x_sample.shape
    depth = norm_mix_pre.shape[0]
    n_ret = w_q.shape[0]
    a_width = w_pool_grp.shape[1] * w_pool_grp.shape[2]
    b_width = sgu_norm_g.shape[1]
    dk = w_q.shape[2] // RET_HEADS
    dv = w_v.shape[2] // RET_HEADS
    d_ff = w_ffn_gate.shape[2]
    m_p, m_s = bp * seq, bs * dec_seq
    m = m_p + m_s
    assert m_s == TM and seq % TM_FFN == 0 and seq % RET_CHUNK == 0 and seq >= POOL_BUF and a_width == b_width
    assert CONV_W - 1 <= dec_seq < POOL_BUF and dec_seq <= SGU_CHUNK and dec_seq <= SUBLANES and m_s <= TM_PROJ
    n_prompt_tiles = m_p // TM
    tiles_per_seq = seq // TM
    geom = dict(n_prompt_tiles=n_prompt_tiles, tiles_per_seq=tiles_per_seq, dec_batch=bs)

    def to_rows(t):
        return t.transpose(1, 0, 2).reshape(t.shape[1] * bs, t.shape[-1])

    def from_rows(r, steps):
        return r.reshape(steps, bs, r.shape[-1]).transpose(1, 0, 2)

    x = (x_prompt.reshape(m_p, d), to_rows(x_sample))

    half = dk // 2
    inv = np.float32(ROPE_BASE) ** (-np.arange(half, dtype=np.float32) / np.float32(half))
    pos_p = np.arange(seq).astype(np.float32)
    pos_s = (PAST_LEN + np.arange(dec_seq)).astype(np.float32)
    ang = np.concatenate([np.tile(pos_p[:, None] * inv[None, :], (bp, 1)),
                          np.repeat(pos_s[:, None] * inv[None, :], bs, axis=0)], axis=0)
    rot = (jnp.asarray(np.cos(ang), dtype=F32), jnp.asarray(np.sin(ang), dtype=F32))
    rot_specs = (pl.BlockSpec((TM_PROJ, half), lambda j, i: (i, 0)),) * 2

    pos = np.arange(seq)
    invc = jnp.asarray(np.concatenate(
        [np.broadcast_to((np.float32(1.0) / np.minimum(pos + 1, w).astype(np.float32))[:, None], (seq, LANES))
         for w in POOL_WINDOWS], axis=1), dtype=F32)

    tril_p = np.tril(np.ones((SGU_CHUNK, SGU_CHUNK), dtype=bool))
    tril_s = np.tril(np.ones((dec_seq, dec_seq), dtype=bool))
    eye_b = np.eye(bs, dtype=np.float32)
    hdim = b_width // SGU_HEADS

    dec_pad = SUBLANES
    tabs_p = _decay_tables(RET_CHUNK, RET_CHUNK, dk, dv)
    tabs_s = _decay_tables(dec_seq, dec_pad, dk, dv)

    def pad_steps(r):
        t = from_rows(r, dec_seq)
        return jnp.pad(t, ((0, 0), (0, dec_pad - dec_seq), (0, 0))).reshape(bs * dec_pad, r.shape[-1])

    conv0_p = jnp.zeros((SUBLANES, d_ff), F32)
    pool_p, pool_s, vn_s, conv_p, conv_s = [], [], [], [], []
    ret_p = ret_s = None
    h = _norm(*x, norm_mix_pre[0])
    for l in range(depth):
        if l % 2 == 0:
            e = l // 2
            pw = dict(layer=e, tn=a_width, n_out=a_width)
            a, w_o = _proj(h, w_mix_in, col0=0, out_dtype=F32, epilogue="none", name="proj_a", side=(w_mix_out, e),
                           **pw)
            u = _proj(h, w_mix_in, col0=1, out_dtype=BF16, epilogue="gelu", name="proj_u", **pw)
            vn, vn_last = _proj(h, w_mix_in, col0=2, out_dtype=BF16, epilogue="gelu_ln",
                                extras=(sgu_norm_g[e].reshape(1, b_width), sgu_norm_b[e].reshape(1, b_width)),
                                extra_specs=(pl.BlockSpec((1, b_width), lambda j, i: (0, 0)),) * 2, name="proj_v",
                                **pw)
            ws_p = jnp.where(tril_p, w_spatial[e][:, :SGU_CHUNK, :SGU_CHUNK], 0.0)
            ws_s = jnp.where(tril_s, w_spatial[e][:, :dec_seq, :dec_seq], 0.0)
            wmix_s = jnp.einsum("hij,bc->hibjc", ws_s, eye_b).reshape(SGU_HEADS, m_s, m_s)
            bias_p = jnp.broadcast_to(b_spatial[e][:, :SGU_CHUNK, None], (SGU_HEADS, SGU_CHUNK, hdim))
            bias_s = jnp.broadcast_to(b_spatial[e][:, :dec_seq, None, None],
                                      (SGU_HEADS, dec_seq, bs, hdim)).reshape(SGU_HEADS, m_s, hdim)
            mixed, pool_new = _mix(a, u, vn, invc, state_pool, w_pool_grp[e].astype(BF16),
                                   pool_scale[e].reshape(1, a_width), ws_p.astype(BF16), bias_p, wmix_s.astype(BF16),
                                   bias_s, layer=e, **geom)
            pool_p.append(jnp.stack([a[(b + 1) * seq - POOL_BUF:(b + 1) * seq] for b in range(bp)]))
            pool_s.append(pool_new)
            vn_s.append(from_rows(vn_last[TM_PROJ - m_s:], dec_seq))
        else:
            r = l // 2
            pw = dict(layer=r, col0=0, tn=1024, out_dtype=BF16)
            q = _proj(h, w_q, n_out=w_q.shape[2], epilogue="rotary", extras=rot, extra_specs=rot_specs,
                      name="proj_q", **pw)
            k = _proj(h, w_k, n_out=w_k.shape[2], epilogue="rotary", extras=rot, extra_specs=rot_specs,
                      scale=dk ** -0.5, name="proj_k", **pw)
            v = _proj(h, w_v, n_out=w_v.shape[2], epilogue="none", name="proj_v_ret", **pw)
            g, w_o = _proj(h, w_g, n_out=w_g.shape[2], epilogue="silu", name="proj_g", side=(w_ret_out, r), **pw)
            gn = ret_norm_g[r].reshape(RET_HEADS, 1, dv)
            qkvg = (q, k, v, g)
            gated, gated_s, ret_p, ret_s = _retention(
                qkvg, tuple(pad_steps(t[m_p:]) for t in qkvg), state_ret, ret_p, ret_s, tabs_p, tabs_s, gn, layer=r,
                n_layers=n_ret, n_seq_p=bp, n_chunk=seq // RET_CHUNK, chunk=RET_CHUNK, n_seq_s=bs, rows_s=dec_pad)
            gated_s = to_rows(gated_s.reshape(bs, dec_pad, -1)[:, :dec_seq])
            mixed = (gated, gated_s)
        x, h = _out_proj(mixed, w_o, x, norm_mix_post[l], norm_ffn_pre[l], tm=TM_OUT, m_first=m_p, name="mix_out")
        wc, bc = w_dconv, b_dconv.reshape(depth, 1, d_ff)
        act_p, tail_p, wgb, wub, wdb = _ffn_in_prompt(h, w_ffn_gate, w_ffn_up, wc, bc, conv0_p, w_ffn_down, layer=l,
                                                      tm=TM_FFN, tn=TN_FFN, n_tiles=m_p // TM_FFN,
                                                      tiles_per_seq=seq // TM_FFN)
        act_s, conv_new = _ffn_in_sample(h, wgb, wub, wc, bc, state_conv, layer=l, tm=m_s, tn=TN_FFN,
                                         row0=m_p // m_s)
        conv_p.append(tail_p.reshape(bp, seq // TM_FFN, SUBLANES, d_ff)[:, -1, SUBLANES - (CONV_W - 1):])
        conv_s.append(conv_new)
        g_next = norm_mix_pre[l + 1] if l + 1 < depth else None
        x, h = _out_proj((act_p, act_s), wdb, x, norm_ffn_post[l], g_next, tm=TM_DOWN, m_first=m_p, name="ffn_out")

    y_prompt = x[0].reshape(bp, seq, d)
    y_sample = from_rows(x[1], dec_seq)
    return (y_prompt, y_sample, jnp.stack(pool_p), jnp.stack(pool_s), jnp.stack(vn_s),
            ret_p, ret_s, jnp.stack(conv_p), jnp.stack(conv_s))
```

```python
import functools

import jax
import jax.numpy as jnp
import numpy as np
from jax import lax
from jax.experimental import pallas as pl
from jax.experimental.pallas import tpu as pltpu

F32 = jnp.float32
BF16 = jnp.bfloat16

PAST_LEN = 16384
POOL_WINDOWS = (2, 4, 8, 16)
POOL_BUF = max(POOL_WINDOWS) - 1
SGU_HEADS = 4
SGU_CHUNK = 128
RET_HEADS = 8
RET_CHUNK = 128
ROPE_BASE = 10000.0
CONV_W = 3
EPS = 1e-6

V7X_VMEM_BYTES = 64 * 1024 * 1024
VMEM_LIMIT_BYTES = V7X_VMEM_BYTES - 4 * 1024 * 1024
SUBLANES = 8
LANES = 128
BF16_SUBLANES = 16

TM = 512
TM_PROJ = 1088
TM_FFN = 1024
TN_FFN = 512
TM_OUT = 256
TM_DOWN = 256


def _cparams(n_axes):
    return pltpu.CompilerParams(
        dimension_semantics=("arbitrary",) * n_axes, vmem_limit_bytes=VMEM_LIMIT_BYTES)


def _resident(shape):
    zeros = (0,) * len(shape)
    return pl.BlockSpec(shape, lambda *_: zeros, pipeline_mode=pl.Buffered(1))


def _rms(x, g):
    return x * lax.rsqrt(jnp.mean(x * x, axis=-1, keepdims=True) + EPS) * g


def _layer_norm(x, g):
    mu = jnp.mean(x, axis=-1, keepdims=True)
    xc = x - mu
    return xc * lax.rsqrt(jnp.mean(xc * xc, axis=-1, keepdims=True) + EPS) * g


def _first_spec(tm, d, n_first):
    return pl.BlockSpec((tm, d), lambda i: (jnp.minimum(i, n_first - 1), 0))


def _second_spec(tm, d, n_first):
    return pl.BlockSpec((tm, d), lambda i: (jnp.maximum(i - n_first, 0), 0))


def _norm_kernel(xp_ref, xs_ref, g_ref, h_ref, *, n_first):
    i = pl.program_id(0)

    @pl.when(i < n_first)
    def _():
        h_ref[...] = _rms(xp_ref[...], g_ref[...]).astype(BF16)

    @pl.when(i >= n_first)
    def _():
        h_ref[...] = _rms(xs_ref[...], g_ref[...]).astype(BF16)


def _norm(xp, xs, g):
    d = xp.shape[1]
    m = xp.shape[0] + xs.shape[0]
    n_first = xp.shape[0] // TM
    return pl.pallas_call(
        functools.partial(_norm_kernel, n_first=n_first),
        grid=(m // TM,),
        in_specs=[_first_spec(TM, d, n_first), _second_spec(TM, d, n_first), _resident((1, d))],
        out_specs=pl.BlockSpec((TM, d), lambda i: (i, 0)),
        out_shape=jax.ShapeDtypeStruct((m, d), BF16),
        compiler_params=_cparams(1),
        name="norm",
    )(xp, xs, g.reshape(1, d))


def _proj_kernel(h_ref, w_ref, *rest, epilogue, scale, side_cast):
    if side_cast:
        *extras, side_ref, o_ref, side_out_ref, wb_ref = rest
        side_out_ref[...] = side_ref[...].astype(BF16)
    else:
        *extras, o_ref, wb_ref = rest

    @pl.when(pl.program_id(1) == 0)
    def _():
        wb_ref[...] = w_ref[...].astype(BF16)

    z = jnp.dot(h_ref[...], wb_ref[...], preferred_element_type=F32)
    if epilogue == "none":
        o_ref[...] = z.astype(o_ref.dtype)
    elif epilogue == "gelu":
        o_ref[...] = jax.nn.gelu(z, approximate=True).astype(o_ref.dtype)
    elif epilogue == "silu":
        o_ref[...] = jax.nn.silu(z).astype(o_ref.dtype)
    elif epilogue == "gelu_ln":
        g_ref, b_ref = extras
        v = jax.nn.gelu(z, approximate=True)
        o_ref[...] = (_layer_norm(v, g_ref[...]) + b_ref[...]).astype(o_ref.dtype)
    elif epilogue == "rotary":
        cos_ref, sin_ref = extras
        c = cos_ref[...]
        s = sin_ref[...]
        half = c.shape[-1]
        for hd in range(z.shape[-1] // (2 * half)):
            lo = hd * 2 * half
            x1 = z[:, lo:lo + half]
            x2 = z[:, lo + half:lo + 2 * half]
            o_ref[:, lo:lo + half] = ((x1 * c - x2 * s) * scale).astype(o_ref.dtype)
            o_ref[:, lo + half:lo + 2 * half] = ((x1 * s + x2 * c) * scale).astype(o_ref.dtype)
    else:
        raise ValueError(epilogue)


def _proj(h, w, *, layer, col0, n_out, tn, out_dtype, epilogue, extras=(), extra_specs=(), scale=1.0, name,
          side=None):
    m, k = h.shape
    assert m % TM_PROJ == 0 and n_out % tn == 0
    n_i = m // TM_PROJ
    in_specs = [pl.BlockSpec((TM_PROJ, k), lambda j, i: (i, 0)),
                pl.BlockSpec((None, k, tn), lambda j, i: (layer, 0, col0 + j)), *extra_specs]
    args = [h, w, *extras]
    out_specs = [pl.BlockSpec((TM_PROJ, tn), lambda j, i: (i, j))]
    out_shape = [jax.ShapeDtypeStruct((m, n_out), out_dtype)]
    if side is not None:
        w2, layer2 = side
        rows = w2.shape[1] // ((n_out // tn) * n_i)
        assert rows * (n_out // tn) * n_i == w2.shape[1] and rows % BF16_SUBLANES == 0
        in_specs.append(pl.BlockSpec((None, rows, w2.shape[2]), lambda j, i: (layer2, j * n_i + i, 0)))
        args.append(w2)
        out_specs.append(pl.BlockSpec((rows, w2.shape[2]), lambda j, i: (j * n_i + i, 0)))
        out_shape.append(jax.ShapeDtypeStruct(w2.shape[1:], BF16))
    res = pl.pallas_call(
        functools.partial(_proj_kernel, epilogue=epilogue, scale=scale, side_cast=side is not None),
        grid=(n_out // tn, n_i),
        in_specs=in_specs,
        out_specs=out_specs,
        out_shape=out_shape,
        scratch_shapes=[pltpu.VMEM((k, tn), BF16)],
        compiler_params=_cparams(2),
        name=name,
    )(*args)
    return res[0] if side is None else tuple(res)


def _out_kernel(*refs, n_first, split_lhs, split_x, last):
    refs = list(refs)
    lhs_refs = [refs.pop(0) for _ in range(2 if split_lhs else 1)]
    wb_ref = refs.pop(0)
    x_refs = [refs.pop(0) for _ in range(2 if split_x else 1)]
    gpost_ref, gnext_ref, *outs = refs
    i = pl.program_id(0)

    def pick(pair):
        return jnp.where(i < n_first, pair[0][...], pair[1][...]) if len(pair) == 2 else pair[0][...]

    y = jnp.dot(pick(lhs_refs), wb_ref[...], preferred_element_type=F32)
    x_new = pick(x_refs) + _rms(y, gpost_ref[...])
    if last:
        yp_ref, ys_ref = outs

        @pl.when(i < n_first)
        def _():
            yp_ref[...] = x_new

        @pl.when(i >= n_first)
        def _():
            ys_ref[...] = x_new
    else:
        xo_ref, h_ref = outs
        xo_ref[...] = x_new
        h_ref[...] = _rms(x_new, gnext_ref[...]).astype(BF16)


def _out_proj(lhs, w, x, g_post, g_next, *, tm, m_first, name):
    k, d = w.shape
    n_first = m_first // tm
    split_lhs, split_x = isinstance(lhs, tuple), isinstance(x, tuple)
    m = sum(t.shape[0] for t in lhs) if split_lhs else lhs.shape[0]
    last = g_next is None

    def pair(width):
        return [_first_spec(tm, width, n_first), _second_spec(tm, width, n_first)]

    def operand(t, width):
        return (list(t), pair(width)) if isinstance(t, tuple) else ([t], [pl.BlockSpec((tm, width), lambda i: (i, 0))])

    lhs_args, lhs_specs = operand(lhs, k)
    x_args, x_specs = operand(x, d)
    if last:
        g_next = g_post
        out_specs = pair(d)
        out_shape = [jax.ShapeDtypeStruct((m_first, d), F32), jax.ShapeDtypeStruct((m - m_first, d), F32)]
    else:
        out_specs = [pl.BlockSpec((tm, d), lambda i: (i, 0))] * 2
        out_shape = [jax.ShapeDtypeStruct((m, d), F32), jax.ShapeDtypeStruct((m, d), BF16)]
    res = pl.pallas_call(
        functools.partial(_out_kernel, n_first=n_first, split_lhs=split_lhs, split_x=split_x, last=last),
        grid=(m // tm,),
        in_specs=[*lhs_specs, _resident((k, d)), *x_specs, _resident((1, d)), _resident((1, d))],
        out_specs=out_specs,
        out_shape=out_shape,
        compiler_params=_cparams(1),
        name=name,
    )(*lhs_args, w, *x_args, g_post.reshape(1, d), g_next.reshape(1, d))
    return ((res[0], res[1]), None) if last else (res[0], res[1])


def _mix_group(a_ref, u_ref, vn_ref, wgrp_ref, pscale_ref, wmix_ref, bias_ref, o_ref, ext_ref,
               inv_cnt, halo_rows, shift, chunk):
    tm, a_width = a_ref.shape
    gdim = a_width // len(POOL_WINDOWS)
    ext_ref[halo_rows:halo_rows + tm, :] = a_ref[...]
    if shift == 1:
        assert POOL_WINDOWS == tuple(2 ** (gi + 1) for gi in range(len(POOL_WINDOWS))) and halo_rows > POOL_BUF
        sums, cur = [], ext_ref[0:halo_rows + tm, :]
        for gi in range(len(POOL_WINDOWS)):
            cur = cur + pltpu.roll(cur, 2 ** gi, axis=0)
            sums.append(cur[halo_rows:halo_rows + tm, 0:gdim])
            cur = cur[:, gdim:]
    else:
        sums = []
        for gi, w in enumerate(POOL_WINDOWS):
            s = ext_ref[halo_rows:halo_rows + tm, gi * gdim:(gi + 1) * gdim]
            for j in range(1, w):
                s = s + ext_ref[halo_rows - j * shift:halo_rows - j * shift + tm, gi * gdim:(gi + 1) * gdim]
            sums.append(s)
    for gi, s in enumerate(sums):
        c0, c1 = gi * gdim, (gi + 1) * gdim
        d = (s * inv_cnt(gi) - a_ref[:, c0:c1]).astype(BF16)
        z = jnp.dot(d, wgrp_ref[gi], preferred_element_type=F32)
        o_ref[:, c0:c1] = (z * pscale_ref[:, c0:c1]).astype(o_ref.dtype)
    hdim = vn_ref.shape[1] // SGU_HEADS
    for c in range(tm // chunk):
        r0, r1 = c * chunk, (c + 1) * chunk
        for hd in range(SGU_HEADS):
            c0, c1 = hd * hdim, (hd + 1) * hdim
            mixed = jnp.dot(wmix_ref[hd], vn_ref[r0:r1, c0:c1].astype(BF16),
                            preferred_element_type=F32) + bias_ref[hd]
            o_ref[r0:r1, a_width + c0:a_width + c1] = (
                u_ref[r0:r1, c0:c1].astype(F32) * mixed).astype(o_ref.dtype)


def _mix_kernel(a_ref, u_ref, vn_ref, invc_ref, pstate_ref, wgrp_ref, pscale_ref,
                wmix_p_ref, bias_p_ref, wmix_s_ref, bias_s_ref, o_ref, pnew_ref, ext_ref,
                *, n_prompt_tiles, tiles_per_seq, dec_batch):
    i = pl.program_id(0)
    tm = a_ref.shape[0]
    gl = LANES
    halo_p = 2 * SUBLANES

    @pl.when(i < n_prompt_tiles)
    def _prompt():
        @pl.when(i % tiles_per_seq == 0)
        def _():
            ext_ref[0:halo_p, :] = jnp.zeros((halo_p, ext_ref.shape[1]), F32)

        def inv_cnt(gi):
            blk = invc_ref[:, gi * gl:(gi + 1) * gl]
            return jnp.concatenate([blk, blk], axis=1)

        _mix_group(a_ref, u_ref, vn_ref, wgrp_ref, pscale_ref, wmix_p_ref, bias_p_ref, o_ref, ext_ref,
                   inv_cnt, halo_p, 1, SGU_CHUNK)
        ext_ref[0:halo_p, :] = ext_ref[tm:tm + halo_p, :]

    @pl.when(i >= n_prompt_tiles)
    def _sample():
        halo_s = POOL_BUF * dec_batch
        for t in range(POOL_BUF):
            ext_ref[t * dec_batch:(t + 1) * dec_batch, :] = pstate_ref[:, t, :]
        _mix_group(a_ref, u_ref, vn_ref, wgrp_ref, pscale_ref, wmix_s_ref, bias_s_ref, o_ref, ext_ref,
                   lambda gi: 1.0 / POOL_WINDOWS[gi], halo_s, dec_batch, tm)
        for t in range(POOL_BUF):
            src = tm + t * dec_batch
            pnew_ref[:, t, :] = ext_ref[src:src + dec_batch, :]


def _mix(a, u, vn, invc, pstate_all, wgrp, pscale, wmix_p, bias_p, wmix_s, bias_s, *, layer, n_prompt_tiles,
         tiles_per_seq, dec_batch):
    m, a_width = a.shape
    pshape = pstate_all.shape[1:]
    b_width = u.shape[1]
    row = lambda i: (i, 0)
    ext_rows = max(2 * SUBLANES, POOL_BUF * dec_batch) + TM
    return pl.pallas_call(
        functools.partial(_mix_kernel, n_prompt_tiles=n_prompt_tiles, tiles_per_seq=tiles_per_seq,
                          dec_batch=dec_batch),
        grid=(m // TM,),
        in_specs=[pl.BlockSpec((TM, a_width), row), pl.BlockSpec((TM, b_width), row),
                  pl.BlockSpec((TM, b_width), row),
                  pl.BlockSpec((TM, invc.shape[1]), lambda i: (jnp.minimum(i, n_prompt_tiles - 1) % tiles_per_seq, 0)),
                  pl.BlockSpec((None,) + pshape, lambda i: (layer, 0, 0, 0), pipeline_mode=pl.Buffered(1)),
                  _resident(wgrp.shape), _resident(pscale.shape),
                  _resident(wmix_p.shape), _resident(bias_p.shape), _resident(wmix_s.shape),
                  _resident(bias_s.shape)],
        out_specs=[pl.BlockSpec((TM, a_width + b_width), row), pl.BlockSpec(pshape, lambda i: (0, 0, 0))],
        out_shape=[jax.ShapeDtypeStruct((m, a_width + b_width), BF16), jax.ShapeDtypeStruct(pshape, F32)],
        scratch_shapes=[pltpu.VMEM((ext_rows, a_width), F32)],
        compiler_params=_cparams(1),
        name="mix",
    )(a, u, vn, invc, pstate_all, wgrp, pscale, wmix_p, bias_p, wmix_s, bias_s)


def _conv_gelu_gate(gate, prev2, prev1, up, wc_ref, bc_ref):
    conv = bc_ref[...] + prev2 * wc_ref[0:1, :]
    conv = conv + prev1 * wc_ref[1:2, :]
    conv = conv + gate * wc_ref[2:3, :]
    return (jax.nn.gelu(conv, approximate=True) * up).astype(BF16)


def _ffn_in_prompt_kernel(h_ref, wg_ref, wu_ref, wc_ref, bc_ref, cstate_ref, wd_ref, act_ref, tail_ref, wgb_ref,
                          wub_ref, wdb_ref, halo_ref, *, tiles_per_seq):
    i = pl.program_id(1)
    wdb_ref[...] = wd_ref[...].astype(BF16)
    tm = h_ref.shape[0]
    halo = cstate_ref.shape[0]

    @pl.when(i == 0)
    def _():
        wgb_ref[...] = wg_ref[...].astype(BF16)
        wub_ref[...] = wu_ref[...].astype(BF16)
        halo_ref[...] = cstate_ref[...]

    h = h_ref[...]
    gate = jnp.dot(h, wgb_ref[...], preferred_element_type=F32)
    up = jnp.dot(h, wub_ref[...], preferred_element_type=F32)
    before = jnp.where((i % tiles_per_seq) == 0, cstate_ref[...], halo_ref[...])
    row = lax.broadcasted_iota(jnp.int32, before.shape, 0)

    def shifted(s):
        r = pltpu.roll(gate, s, axis=0)
        first = jnp.where(row < s, pltpu.roll(before, s, axis=0), r[0:halo])
        return jnp.concatenate([first, r[halo:]], axis=0)

    act_ref[...] = _conv_gelu_gate(gate, shifted(2), shifted(1), up, wc_ref, bc_ref)
    tail = gate[tm - halo:tm, :]
    tail_ref[...] = tail
    halo_ref[...] = tail


def _ffn_in_sample_kernel(h_ref, wgb_ref, wub_ref, wc_ref, bc_ref, cstate_ref, act_ref, cnew_ref, ext_ref):
    tm = h_ref.shape[0]
    shift, n_state = cstate_ref.shape[0], cstate_ref.shape[1]
    halo = n_state * shift
    h = h_ref[...]
    gate = jnp.dot(h, wgb_ref[...], preferred_element_type=F32)
    up = jnp.dot(h, wub_ref[...], preferred_element_type=F32)
    for t in range(n_state):
        ext_ref[t * shift:(t + 1) * shift, :] = cstate_ref[:, t, :]
    ext_ref[halo:halo + tm, :] = gate
    act_ref[...] = _conv_gelu_gate(gate, ext_ref[halo - 2 * shift:halo - 2 * shift + tm, :],
                                   ext_ref[halo - shift:halo - shift + tm, :], up, wc_ref, bc_ref)
    for t in range(n_state):
        cnew_ref[:, t, :] = gate[tm - halo + t * shift:tm - halo + (t + 1) * shift, :]


def _ffn_in_prompt(h, wg, wu, wc, bc, cstate, wd, *, layer, tm, tn, n_tiles, tiles_per_seq):
    k = h.shape[1]
    n = wg.shape[2]
    halo = cstate.shape[0]
    steps = (n // tn) * n_tiles
    wd_rows = wd.shape[1] // steps
    assert wd_rows * steps == wd.shape[1] and wd_rows % BF16_SUBLANES == 0
    d_out = wd.shape[2]
    wspec = pl.BlockSpec((None, k, tn), lambda j, i: (layer, 0, j))
    wbspec = pl.BlockSpec((k, tn), lambda j, i: (0, j))
    return pl.pallas_call(
        functools.partial(_ffn_in_prompt_kernel, tiles_per_seq=tiles_per_seq),
        grid=(n // tn, n_tiles),
        in_specs=[pl.BlockSpec((tm, k), lambda j, i: (i, 0)), wspec, wspec,
                  pl.BlockSpec((None, CONV_W, tn), lambda j, i: (layer, 0, j)),
                  pl.BlockSpec((None, 1, tn), lambda j, i: (layer, 0, j)),
                  pl.BlockSpec((halo, tn), lambda j, i: (0, j)),
                  pl.BlockSpec((None, wd_rows, d_out), lambda j, i: (layer, j * n_tiles + i, 0))],
        out_specs=[pl.BlockSpec((tm, tn), lambda j, i: (i, j)),
                   pl.BlockSpec((halo, tn), lambda j, i: (i // tiles_per_seq, j)),
                   wbspec, wbspec, pl.BlockSpec((wd_rows, d_out), lambda j, i: (j * n_tiles + i, 0))],
        out_shape=[jax.ShapeDtypeStruct((n_tiles * tm, n), BF16),
                   jax.ShapeDtypeStruct((n_tiles // tiles_per_seq * halo, n), F32),
                   jax.ShapeDtypeStruct((k, n), BF16), jax.ShapeDtypeStruct((k, n), BF16),
                   jax.ShapeDtypeStruct(wd.shape[1:], BF16)],
        scratch_shapes=[pltpu.VMEM((halo, tn), F32)],
        compiler_params=_cparams(2),
        name="ffn_in_prompt",
    )(h, wg, wu, wc, bc, cstate, wd)


def _ffn_in_sample(h, wgb, wub, wc, bc, cstate_all, *, layer, tm, tn, row0):
    k = h.shape[1]
    n = wgb.shape[1]
    bs, n_state = cstate_all.shape[1:3]
    wbspec = pl.BlockSpec((k, tn), lambda j: (0, j))
    return pl.pallas_call(
        _ffn_in_sample_kernel,
        grid=(n // tn,),
        in_specs=[pl.BlockSpec((tm, k), lambda j: (row0, 0)), wbspec, wbspec,
                  pl.BlockSpec((None, CONV_W, tn), lambda j: (layer, 0, j)),
                  pl.BlockSpec((None, 1, tn), lambda j: (layer, 0, j)),
                  pl.BlockSpec((None, bs, n_state, tn), lambda j: (layer, 0, 0, j))],
        out_specs=[pl.BlockSpec((tm, tn), lambda j: (0, j)), pl.BlockSpec((bs, n_state, tn), lambda j: (0, 0, j))],
        out_shape=[jax.ShapeDtypeStruct((tm, n), BF16), jax.ShapeDtypeStruct((bs, n_state, n), F32)],
        scratch_shapes=[pltpu.VMEM((n_state * bs + tm, tn), F32)],
        compiler_params=_cparams(1),
        name="ffn_in_sample",
    )(h, wgb, wub, wc, bc, cstate_all)


def _ret_head(hd, q_ref, k_ref, v_ref, g_ref, s_in_ref, s_out_ref, dmask_ref, xi_ref, zeta_ref, gc_ref, gn_ref):
    dk = q_ref.shape[1] // RET_HEADS
    dv = v_ref.shape[1] // RET_HEADS
    qh = q_ref[:, hd * dk:(hd + 1) * dk]
    kh = k_ref[:, hd * dk:(hd + 1) * dk]
    vh = v_ref[:, hd * dv:(hd + 1) * dv]
    state = s_in_ref[0, hd]
    sc = lax.dot_general(qh, kh, (((1,), (1,)), ((), ())), preferred_element_type=F32) * dmask_ref[hd]
    o = jnp.dot(sc.astype(BF16), vh, preferred_element_type=F32)
    o = o + jnp.dot(qh, state.astype(BF16), preferred_element_type=F32) * xi_ref[hd]
    kz = (kh.astype(F32) * zeta_ref[hd]).astype(BF16)
    s_out_ref[0, hd] = gc_ref[hd] * state + lax.dot_general(
        kz, vh, (((0,), (0,)), ((), ())), preferred_element_type=F32)
    on = _layer_norm(o, gn_ref[hd])
    return (g_ref[:, hd * dv:(hd + 1) * dv].astype(F32) * on).astype(BF16)


def _ret_kernel(qp_ref, kp_ref, vp_ref, gp_ref, qs_ref, ks_ref, vs_ref, gs_ref, s0s_ref,
                dmask_p_ref, xi_p_ref, zeta_p_ref, gc_p_ref, dmask_s_ref, xi_s_ref, zeta_s_ref, gc_s_ref, gn_ref,
                *rest):
    op_ref, os_ref, sp_ref, ss_ref = rest[-4:]

    @pl.when(pl.program_id(1) == 0)
    def _():
        sp_ref[...] = jnp.zeros(sp_ref.shape, F32)

    dv = vp_ref.shape[1] // RET_HEADS
    n_s = s0s_ref.shape[0]
    rows_s = qs_ref.shape[0] // n_s
    for hd in range(RET_HEADS):
        cols = slice(hd * dv, (hd + 1) * dv)
        op_ref[:, cols] = _ret_head(hd, qp_ref, kp_ref, vp_ref, gp_ref, sp_ref, sp_ref,
                                    dmask_p_ref, xi_p_ref, zeta_p_ref, gc_p_ref, gn_ref)
        for j in range(n_s):
            rows = pl.ds(j * rows_s, rows_s)
            os_ref[j * rows_s:(j + 1) * rows_s, cols] = _ret_head(
                hd, qs_ref.at[rows], ks_ref.at[rows], vs_ref.at[rows], gs_ref.at[rows], s0s_ref.at[j:j + 1],
                ss_ref.at[j:j + 1], dmask_s_ref, xi_s_ref, zeta_s_ref, gc_s_ref, gn_ref)


def _retention(qkvg_p, qkvg_s, s0_s_all, sp_prev, ss_prev, tabs_p, tabs_s, gn, *, layer, n_layers, n_seq_p, n_chunk,
               chunk, n_seq_s, rows_s):
    steps = n_seq_p * n_chunk
    per_step = n_seq_s // steps
    assert per_step * steps == n_seq_s
    dq, dvv = qkvg_p[0].shape[1], qkvg_p[2].shape[1]
    state_shape = (RET_HEADS, dq // RET_HEADS, dvv // RET_HEADS)
    blk_p = lambda b, c: (b * n_chunk + c, 0)
    blk_s = lambda b, c: (b * n_chunk + c, 0)
    sp_spec = pl.BlockSpec((None, 1) + state_shape, lambda b, c: (layer, b, 0, 0, 0))
    ss_spec = pl.BlockSpec((None, per_step) + state_shape, lambda b, c: (layer, b * n_chunk + c, 0, 0, 0))
    in_specs = [pl.BlockSpec((chunk, t.shape[1]), blk_p) for t in qkvg_p]
    in_specs += [pl.BlockSpec((per_step * rows_s, t.shape[1]), blk_s) for t in qkvg_s]
    in_specs.append(ss_spec)
    args = [*qkvg_p, *qkvg_s, s0_s_all]
    for t in (*tabs_p, *tabs_s, gn):
        in_specs.append(_resident(t.shape))
        args.append(t)
    aliases = {}
    for prev, out_idx in ((sp_prev, 2), (ss_prev, 3)):
        if prev is not None:
            in_specs.append(pl.BlockSpec(memory_space=pl.ANY))
            args.append(prev)
            aliases[len(args) - 1] = out_idx
    return pl.pallas_call(
        _ret_kernel,
        grid=(n_seq_p, n_chunk),
        in_specs=in_specs,
        out_specs=[pl.BlockSpec((chunk, dvv), blk_p), pl.BlockSpec((per_step * rows_s, dvv), blk_s), sp_spec, ss_spec],
        out_shape=[jax.ShapeDtypeStruct((steps * chunk, dvv), BF16),
                   jax.ShapeDtypeStruct((n_seq_s * rows_s, dvv), BF16),
                   jax.ShapeDtypeStruct((n_layers, n_seq_p) + state_shape, F32),
                   jax.ShapeDtypeStruct((n_layers, n_seq_s) + state_shape, F32)],
        input_output_aliases=aliases,
        compiler_params=_cparams(2),
        name="retention",
    )(*args)


def _decay_tables(c_true, c_pad, dk, dv):
    f32 = np.float32
    lg = np.log1p(-np.exp2(f32(-5.0) - np.arange(RET_HEADS, dtype=f32)))
    idx = np.arange(c_true, dtype=f32)
    diff = idx[:, None] - idx[None, :]
    dmask = np.where(diff >= 0, np.exp(lg[:, None, None] * np.maximum(diff, f32(0.0))), f32(0.0))
    xi = np.exp(lg[:, None] * (idx + f32(1.0)))
    zeta = np.exp(lg[:, None] * (f32(c_true - 1.0) - idx))
    g_c = np.exp(lg * f32(c_true))
    pad = c_pad - c_true
    dmask = np.pad(dmask, ((0, 0), (0, pad), (0, pad)))
    xi = np.pad(xi, ((0, 0), (0, pad)))
    zeta = np.pad(zeta, ((0, 0), (0, pad)))
    tabs = (dmask,
            np.broadcast_to(xi[:, :, None], (RET_HEADS, c_pad, dv)),
            np.broadcast_to(zeta[:, :, None], (RET_HEADS, c_pad, dk)),
            np.broadcast_to(g_c[:, None, None], (RET_HEADS, 1, dv)))
    return tuple(jnp.asarray(t, dtype=F32) for t in tabs)


def kernel(x_prompt, x_sample, state_pool, state_ret, state_conv, w_mix_in, w_pool_grp, pool_scale, w_spatial,
           b_spatial, sgu_norm_g, sgu_norm_b, w_mix_out, w_q, w_k, w_v, w_g, ret_norm_g, w_ret_out, norm_mix_pre,
           norm_mix_post, norm_ffn_pre, norm_ffn_post, w_ffn_gate, w_ffn_up, w_dconv, b_dconv, w_ffn_down):
    bp, seq, d = x_prompt.shape
    bs, dec_seq, _ = x_sample.shape
    depth = norm_mix_pre.shape[0]
    n_ret = w_q.shape[0]
    a_width = w_pool_grp.shape[1] * w_pool_grp.shape[2]
    b_width = sgu_norm_g.shape[1]
    dk = w_q.shape[2] // RET_HEADS
    dv = w_v.shape[2] // RET_HEADS
    d_ff = w_ffn_gate.shape[2]
    m_p, m_s = bp * seq, bs * dec_seq
    m = m_p + m_s
    assert m_s == TM and seq % TM_FFN == 0 and seq % RET_CHUNK == 0 and seq >= POOL_BUF and a_width == b_width
    assert CONV_W - 1 <= dec_seq < POOL_BUF and dec_seq <= SGU_CHUNK and dec_seq <= SUBLANES
    n_prompt_tiles = m_p // TM
    tiles_per_seq = seq // TM
    geom = dict(n_prompt_tiles=n_prompt_tiles, tiles_per_seq=tiles_per_seq, dec_batch=bs)

    def to_rows(t):
        return t.transpose(1, 0, 2).reshape(t.shape[1] * bs, t.shape[-1])

    def from_rows(r, steps):
        return r.reshape(steps, bs, r.shape[-1]).transpose(1, 0, 2)

    x = (x_prompt.reshape(m_p, d), to_rows(x_sample))

    half = dk // 2
    inv = np.float32(ROPE_BASE) ** (-np.arange(half, dtype=np.float32) / np.float32(half))
    pos_p = np.arange(seq).astype(np.float32)
    pos_s = (PAST_LEN + np.arange(dec_seq)).astype(np.float32)
    ang = np.concatenate([np.tile(pos_p[:, None] * inv[None, :], (bp, 1)),
                          np.repeat(pos_s[:, None] * inv[None, :], bs, axis=0)], axis=0)
    rot = (jnp.asarray(np.cos(ang), dtype=F32), jnp.asarray(np.sin(ang), dtype=F32))
    rot_specs = (pl.BlockSpec((TM_PROJ, half), lambda j, i: (i, 0)),) * 2

    pos = np.arange(seq)
    invc = jnp.asarray(np.concatenate(
        [np.broadcast_to((np.float32(1.0) / np.minimum(pos + 1, w).astype(np.float32))[:, None], (seq, LANES))
         for w in POOL_WINDOWS], axis=1), dtype=F32)

    tril_p = np.tril(np.ones((SGU_CHUNK, SGU_CHUNK), dtype=bool))
    tril_s = np.tril(np.ones((dec_seq, dec_seq), dtype=bool))
    eye_b = np.eye(bs, dtype=np.float32)
    hdim = b_width // SGU_HEADS

    dec_pad = SUBLANES
    tabs_p = _decay_tables(RET_CHUNK, RET_CHUNK, dk, dv)
    tabs_s = _decay_tables(dec_seq, dec_pad, dk, dv)

    def pad_steps(r):
        t = from_rows(r, dec_seq)
        return jnp.pad(t, ((0, 0), (0, dec_pad - dec_seq), (0, 0))).reshape(bs * dec_pad, r.shape[-1])

    conv0_p = jnp.zeros((SUBLANES, d_ff), F32)
    pool_p, pool_s, vn_s, conv_p, conv_s = [], [], [], [], []
    ret_p = ret_s = None
    h = _norm(*x, norm_mix_pre[0])
    for l in range(depth):
        if l % 2 == 0:
            e = l // 2
            pw = dict(layer=e, tn=a_width, n_out=a_width)
            a, w_o = _proj(h, w_mix_in, col0=0, out_dtype=F32, epilogue="none", name="proj_a", side=(w_mix_out, e),
                           **pw)
            u = _proj(h, w_mix_in, col0=1, out_dtype=BF16, epilogue="gelu", name="proj_u", **pw)
            vn = _proj(h, w_mix_in, col0=2, out_dtype=F32, epilogue="gelu_ln",
                       extras=(sgu_norm_g[e].reshape(1, b_width), sgu_norm_b[e].reshape(1, b_width)),
                       extra_specs=(pl.BlockSpec((1, b_width), lambda j, i: (0, 0)),) * 2, name="proj_v", **pw)
            ws_p = jnp.where(tril_p, w_spatial[e][:, :SGU_CHUNK, :SGU_CHUNK], 0.0)
            ws_s = jnp.where(tril_s, w_spatial[e][:, :dec_seq, :dec_seq], 0.0)
            wmix_s = jnp.einsum("hij,bc->hibjc", ws_s, eye_b).reshape(SGU_HEADS, m_s, m_s)
            bias_p = jnp.broadcast_to(b_spatial[e][:, :SGU_CHUNK, None], (SGU_HEADS, SGU_CHUNK, hdim))
            bias_s = jnp.broadcast_to(b_spatial[e][:, :dec_seq, None, None],
                                      (SGU_HEADS, dec_seq, bs, hdim)).reshape(SGU_HEADS, m_s, hdim)
            mixed, pool_new = _mix(a, u, vn, invc, state_pool, w_pool_grp[e].astype(BF16),
                                   pool_scale[e].reshape(1, a_width), ws_p.astype(BF16), bias_p, wmix_s.astype(BF16),
                                   bias_s, layer=e, **geom)
            pool_p.append(jnp.stack([a[(b + 1) * seq - POOL_BUF:(b + 1) * seq] for b in range(bp)]))
            pool_s.append(pool_new)
            vn_s.append(from_rows(vn[m_p:], dec_seq))
        else:
            r = l // 2
            pw = dict(layer=r, col0=0, tn=1024, out_dtype=BF16)
            q = _proj(h, w_q, n_out=w_q.shape[2], epilogue="rotary", extras=rot, extra_specs=rot_specs,
                      name="proj_q", **pw)
            k = _proj(h, w_k, n_out=w_k.shape[2], epilogue="rotary", extras=rot, extra_specs=rot_specs,
                      scale=dk ** -0.5, name="proj_k", **pw)
            v = _proj(h, w_v, n_out=w_v.shape[2], epilogue="none", name="proj_v_ret", **pw)
            g, w_o = _proj(h, w_g, n_out=w_g.shape[2], epilogue="silu", name="proj_g", side=(w_ret_out, r), **pw)
            gn = ret_norm_g[r].reshape(RET_HEADS, 1, dv)
            qkvg = (q, k, v, g)
            gated, gated_s, ret_p, ret_s = _retention(
                qkvg, tuple(pad_steps(t[m_p:]) for t in qkvg), state_ret, ret_p, ret_s, tabs_p, tabs_s, gn, layer=r,
                n_layers=n_ret, n_seq_p=bp, n_chunk=seq // RET_CHUNK, chunk=RET_CHUNK, n_seq_s=bs, rows_s=dec_pad)
            gated_s = to_rows(gated_s.reshape(bs, dec_pad, -1)[:, :dec_seq])
            mixed = (gated, gated_s)
        x, h = _out_proj(mixed, w_o, x, norm_mix_post[l], norm_ffn_pre[l], tm=TM_OUT, m_first=m_p, name="mix_out")
        wc, bc = w_dconv, b_dconv.reshape(depth, 1, d_ff)
        act_p, tail_p, wgb, wub, wdb = _ffn_in_prompt(h, w_ffn_gate, w_ffn_up, wc, bc, conv0_p, w_ffn_down, layer=l,
                                                      tm=TM_FFN, tn=TN_FFN, n_tiles=m_p // TM_FFN,
                                                      tiles_per_seq=seq // TM_FFN)
        act_s, conv_new = _ffn_in_sample(h, wgb, wub, wc, bc, state_conv, layer=l, tm=m_s, tn=TN_FFN,
                                         row0=m_p // m_s)
        conv_p.append(tail_p.reshape(bp, SUBLANES, d_ff)[:, SUBLANES - (CONV_W - 1):])
        conv_s.append(conv_new)
        g_next = norm_mix_pre[l + 1] if l + 1 < depth else None
        x, h = _out_proj((act_p, act_s), wdb, x, norm_ffn_post[l], g_next, tm=TM_DOWN, m_first=m_p, name="ffn_out")

    y_prompt = x[0].reshape(bp, seq, d)
    y_sample = from_rows(x[1], dec_seq)
    return (y_prompt, y_sample, jnp.stack(pool_p), jnp.stack(pool_s), jnp.stack(vn_s),
            ret_p, ret_s, jnp.stack(conv_p), jnp.stack(conv_s))
```
